```python
import math
import jax, jax.numpy as jnp
from jax import lax
import numpy as np


D_MODEL = 2048
BATCH = 2
SEQ = 16384
DEPTH = 2

CTX_LEN = 256
GRID_W = 64
D_MIX = D_MODEL
NA_WIDTH = D_MIX // 4
NA_HEADS = 8
NA_HEAD_DIM = NA_WIDTH // NA_HEADS
NA_ROWS = 8
NA_COLS = 16
ML_WIDTH = D_MIX // 4
ML_HEADS = 4
ML_HEAD_DIM = ML_WIDTH // ML_HEADS
ML_CHUNK = 64
ROPE_BASE = 10000.0
SC_WIDTH = D_MIX // 4
SC_GROUPS = 8
HY_WIDTH = D_MIX - NA_WIDTH - ML_WIDTH - SC_WIDTH
HY_GROUPS = 8
HY_ORDER = 2
HY_BANDS = 8
HY_EMB = 1 + 2 * HY_BANDS
HY_FFN = 64
HY_FAST_DECAY = 0.3
HY_SLOW_DECAY = 1.5
HY_TARGET = 0.01
N_EXPERTS = 32
N_GROUPS = 8
EXPERTS_PER_GROUP = N_EXPERTS // N_GROUPS
TOP_K = 2
EXPERT_FF = 1408
MOE_BLOCK = 128
NORM_EPS = 1e-6
NA_IN = 3 * NA_WIDTH
ML_IN = 4 * ML_WIDTH + 4 * ML_HEADS
SC_IN = 3 * SC_WIDTH
HY_IN = 3 * HY_WIDTH
D_IN = NA_IN + ML_IN + SC_IN + HY_IN

kernel_name = 'hybrid_grid_diffusion_block'


def rms_norm(x, gain):
    xf = x.astype(jnp.float32)
    y = xf * lax.rsqrt(jnp.mean(xf * xf, axis=-1, keepdims=True) + NORM_EPS)
    return (y * gain.astype(jnp.float32)).astype(x.dtype)


def group_rms_norm(y, n_groups, gain):
    shp = y.shape
    yg = y.reshape(shp[:-1] + (n_groups, shp[-1] // n_groups))
    return rms_norm(yg, gain.reshape(n_groups, -1)).reshape(shp)


def modulate(h, shift, scale):
    return h * (1 + scale) + shift


def dwconv3(u, w, b=None):
    up = jnp.pad(u, ((0, 0), (1, 1), (0, 0)))
    y = up[:, :-2] * w[0] + up[:, 1:-1] * w[1] + up[:, 2:] * w[2]
    return y if b is None else y + b


def axial_rope(t):
    L, hd = t.shape[1], t.shape[-1]
    half, quarter = hd // 2, hd // 4
    pos = jnp.arange(L, dtype=jnp.int32)
    inv = ROPE_BASE ** (-jnp.arange(quarter, dtype=jnp.float32) / quarter)

    def rot(xa, p):
        ang = p.astype(jnp.float32)[:, None] * inv[None, :]
        cos = jnp.cos(ang)[:, None, :].astype(t.dtype)
        sin = jnp.sin(ang)[:, None, :].astype(t.dtype)
        x1, x2 = xa[..., :quarter], xa[..., quarter:]
        return jnp.concatenate([x1 * cos - x2 * sin, x1 * sin + x2 * cos], axis=-1)

    return jnp.concatenate([rot(t[..., :half], pos // GRID_W), rot(t[..., half:], pos % GRID_W)], axis=-1)


def neighbourhood_attention(q, k, v, kc, vc, rpb):
    B, N, H, dh = q.shape
    rows = N // GRID_W
    wr = min(NA_ROWS, rows)
    scale = dh ** -0.5
    qg = q.reshape(B, rows, GRID_W, H, dh)
    kg = k.reshape(B, rows, GRID_W, H, dh)
    vg = v.reshape(B, rows, GRID_W, H, dh)
    cols = np.arange(GRID_W)
    col_idx = np.clip(cols - NA_COLS // 2, 0, GRID_W - NA_COLS)[:, None] + np.arange(NA_COLS)[None, :]
    bias_col = rpb[:, :, col_idx - cols[:, None] + NA_COLS - 1]
    n_loc = wr * NA_COLS

    def row_block(r):
        r0 = jnp.clip(r - wr // 2, 0, rows - wr)
        q_r = lax.dynamic_index_in_dim(qg, r, axis=1, keepdims=False)
        kb = lax.dynamic_slice_in_dim(kg, r0, wr, axis=1)[:, :, col_idx]
        vb = lax.dynamic_slice_in_dim(vg, r0, wr, axis=1)[:, :, col_idx]
        bias = jnp.take(bias_col, r0 - r + jnp.arange(wr) + NA_ROWS - 1, axis=1)
        s_loc = jnp.einsum('bqhd,biqjhd->bhqij', q_r, kb).astype(jnp.float32) * scale + jnp.transpose(bias, (0, 2, 1, 3)).astype(jnp.float32)
        s_ctx = jnp.einsum('bqhd,bkhd->bhqk', q_r, kc).astype(jnp.float32) * scale
        p = jax.nn.softmax(jnp.concatenate([s_loc.reshape(B, H, GRID_W, n_loc), s_ctx], axis=-1), axis=-1).astype(v.dtype)
        p_loc = p[..., :n_loc].reshape(B, H, GRID_W, wr, NA_COLS)
        return jnp.einsum('bhqij,biqjhd->bqhd', p_loc, vb) + jnp.einsum('bhqk,bkhd->bqhd', p[..., n_loc:], vc)

    out = lax.map(row_block, jnp.arange(rows))
    return jnp.moveaxis(out, 0, 1).reshape(B, N, H * dh)


def context_attention(qc, kc, vc):
    B, L, H, dh = qc.shape
    s = jnp.einsum('bqhd,bkhd->bhqk', qc, kc).astype(jnp.float32) * dh ** -0.5
    p = jax.nn.softmax(s, axis=-1).astype(vc.dtype)
    return jnp.einsum('bhqk,bkhd->bqhd', p, vc).reshape(B, L, H * dh)


def na_mixer(pa, pca, q_gain, k_gain, rpb, need_ctx):
    W = NA_WIDTH

    def heads(t):
        return t.reshape(t.shape[:2] + (NA_HEADS, NA_HEAD_DIM))

    q = rms_norm(heads(pa[..., :W]), q_gain)
    k = rms_norm(heads(pa[..., W:2 * W]), k_gain)
    v = heads(pa[..., 2 * W:])
    kc = rms_norm(heads(pca[..., W:2 * W]), k_gain)
    vc = heads(pca[..., 2 * W:])
    y = neighbourhood_attention(q, k, v, kc, vc, rpb)
    yc = None
    if need_ctx:
        qc = rms_norm(heads(pca[..., :W]), q_gain)
        yc = context_attention(qc, kc, vc)
    return y, yc


def mlstm_chunk_scan(q, k, v, log_i, log_f, state):
    B, L, H, dh = q.shape
    nc = L // ML_CHUNK

    def to_chunks(t):
        t = t.reshape((B, nc, ML_CHUNK) + t.shape[2:])
        return jnp.moveaxis(jnp.moveaxis(t, 1, 0), 3, 2)

    mask = jnp.tril(jnp.ones((ML_CHUNK, ML_CHUNK), dtype=bool))

    def step(carry, inp):
        C, n, m = carry
        qc, kc, vc, li, lf = inp
        b = jnp.cumsum(lf, axis=-1)
        dmat = jnp.where(mask, b[..., :, None] - b[..., None, :] + li[..., None, :], -jnp.inf)
        inter = b + m[..., None]
        m_t = jnp.maximum(jnp.max(dmat, axis=-1), inter)
        s = jnp.einsum('bhtd,bhsd->bhts', qc, kc) * jnp.exp(dmat - m_t[..., None])
        g = jnp.exp(inter - m_t)
        num = jnp.einsum('bhts,bhsd->bhtd', s, vc) + g[..., None] * jnp.einsum('bhvk,bhtk->bhtv', C, qc)
        den = jnp.sum(s, axis=-1) + g * jnp.einsum('bhk,bhtk->bht', n, qc)
        h = num / jnp.maximum(jnp.abs(den), jnp.exp(-m_t))[..., None]
        b_last = b[..., -1]
        m_new = jnp.maximum(b_last + m, jnp.max(b_last[..., None] - b + li, axis=-1))
        w_s = jnp.exp(b_last[..., None] - b + li - m_new[..., None])
        decay = jnp.exp(b_last + m - m_new)
        C_new = decay[..., None, None] * C + jnp.einsum('bhsv,bhsk->bhvk', w_s[..., None] * vc, kc)
        n_new = decay[..., None] * n + jnp.einsum('bhs,bhsk->bhk', w_s, kc)
        return (C_new, n_new, m_new), h

    xs = (to_chunks(q), to_chunks(k), to_chunks(v), to_chunks(log_i), to_chunks(log_f))
    state, h = lax.scan(step, state, xs)
    h = jnp.swapaxes(jnp.moveaxis(h, 0, 1), 2, 3).reshape(B, L, H, dh)
    return state, h


def mlstm_mixer(pm, pcm, conv_w, conv_b, gate_b, need_ctx):
    W, H, dh = ML_WIDTH, ML_HEADS, ML_HEAD_DIM

    def prep(pp, rope):
        B, L = pp.shape[:2]
        qk = jax.nn.silu(dwconv3(pp[..., :2 * W], conv_w, conv_b))
        q = qk[..., :W].reshape(B, L, H, dh)
        k = qk[..., W:].reshape(B, L, H, dh)
        if rope:
            q, k = axial_rope(q), axial_rope(k)
        v = pp[..., 2 * W:3 * W].reshape(B, L, H, dh)
        gates = (pp[..., 4 * W:] + gate_b).astype(jnp.float32).reshape(B, L, 4, H)
        return q.astype(jnp.float32), (k * dh ** -0.5).astype(jnp.float32), v.astype(jnp.float32), gates

    q, k, v, g = prep(pm, True)
    qc, kc, vc, gc = prep(pcm, False)
    B, N = pm.shape[:2]
    zero = (jnp.zeros((B, H, dh, dh), jnp.float32), jnp.zeros((B, H, dh), jnp.float32), jnp.zeros((B, H), jnp.float32))

    def flip(t):
        return jnp.flip(t, axis=1)

    st_f, hc_f = mlstm_chunk_scan(qc, kc, vc, gc[:, :, 0], jax.nn.log_sigmoid(gc[:, :, 1]), zero)
    _, h_f = mlstm_chunk_scan(q, k, v, g[:, :, 0], jax.nn.log_sigmoid(g[:, :, 1]), st_f)
    st_b, hc_b = mlstm_chunk_scan(flip(qc), flip(kc), flip(vc), flip(gc[:, :, 2]), flip(jax.nn.log_sigmoid(gc[:, :, 3])), zero)
    _, h_b = mlstm_chunk_scan(flip(q), flip(k), flip(v), flip(g[:, :, 2]), flip(jax.nn.log_sigmoid(g[:, :, 3])), st_b)
    y = (jax.nn.sigmoid(pm[..., 3 * W:4 * W].astype(jnp.float32)) * (h_f + flip(h_b)).reshape(B, N, W)).astype(pm.dtype)
    yc = None
    if need_ctx:
        Lc = pcm.shape[1]
        yc = (jax.nn.sigmoid(pcm[..., 3 * W:4 * W].astype(jnp.float32)) * (hc_f + flip(hc_b)).reshape(B, Lc, W)).astype(pcm.dtype)
    return y, yc


def short_conv_mixer(ps, w):
    W = SC_WIDTH
    bg, cg, xin = ps[..., :W], ps[..., W:2 * W], ps[..., 2 * W:]
    return bg * dwconv3(cg * xin, w)


def hyena_filter_fft(L, w1, b1, w2, b2, w3, freq):
    f32 = jnp.float32
    pos = jnp.arange(L, dtype=f32)
    t = (pos / L)[:, None]
    bands = jnp.linspace(1e-4, HY_BANDS - 1, HY_BANDS, dtype=f32)
    ang = 2 * math.pi * t * bands
    z = jnp.concatenate([t, jnp.cos(ang), jnp.sin(ang)], axis=-1)
    hdn = jnp.sin(freq[0].astype(f32) * (z @ w1.astype(f32) + b1.astype(f32)))
    hdn = jnp.sin(freq[1].astype(f32) * (hdn @ w2.astype(f32) + b2.astype(f32)))
    h = hdn @ w3.astype(f32)
    deltas = jnp.abs(jnp.linspace(math.log(HY_TARGET) / HY_SLOW_DECAY, math.log(HY_TARGET) / HY_FAST_DECAY, HY_WIDTH, dtype=f32))
    decay = jnp.exp(-t * deltas)
    h = h.reshape(L, HY_ORDER, 2, HY_WIDTH) * decay[:, None, None, :]
    kern = jnp.concatenate([h[:, :, 0], jnp.zeros((1, HY_ORDER, HY_WIDTH), f32), h[:0:-1, :, 1]], axis=0)
    return jnp.fft.rfft(kern, axis=0)


def fft_long_conv(u, kf, bias):
    L = u.shape[1]
    uf = jnp.fft.rfft(u.astype(jnp.float32), n=2 * L, axis=1)
    y = jnp.fft.irfft(uf * kf[None], n=2 * L, axis=1)[:, :L]
    return (y + u.astype(jnp.float32) * bias.astype(jnp.float32)).astype(u.dtype)


def hyena_mixer(ph, conv_w, conv_b, kf, bias):
    W = HY_WIDTH
    z = dwconv3(ph, conv_w, conv_b)
    y = z[..., :W]
    for o in range(HY_ORDER):
        y = z[..., (o + 1) * W:(o + 2) * W] * fft_long_conv(y, kf[:, o], bias[o])
    return y


def merge_heads(parts, gain):
    groups = (NA_HEADS, ML_HEADS, SC_GROUPS, HY_GROUPS)
    widths = (NA_WIDTH, ML_WIDTH, SC_WIDTH, HY_WIDTH)
    outs = []
    off = 0
    for part, ng, wd in zip(parts, groups, widths):
        outs.append(group_rms_norm(part, ng, gain[off:off + wd]))
        off += wd
    return jnp.concatenate(outs, axis=-1)


def hybrid_mixer(p, pc, na_q_gain, na_k_gain, na_rpb, ml_conv_w, ml_conv_b, ml_gate_b, sc_w, hy_conv_w, hy_conv_b, hy_w1, hy_b1, hy_w2, hy_b2, hy_w3, hy_freq, hy_bias, out_gain, need_ctx):
    o1 = NA_IN
    o2 = o1 + ML_IN
    o3 = o2 + SC_IN
    y_na, yc_na = na_mixer(p[..., :o1], pc[..., :o1], na_q_gain, na_k_gain, na_rpb, need_ctx)
    y_ml, yc_ml = mlstm_mixer(p[..., o1:o2], pc[..., o1:o2], ml_conv_w, ml_conv_b, ml_gate_b, need_ctx)
    y_sc = short_conv_mixer(p[..., o2:o3], sc_w)
    kf = hyena_filter_fft(p.shape[1], hy_w1, hy_b1, hy_w2, hy_b2, hy_w3, hy_freq)
    y_hy = hyena_mixer(p[..., o3:], hy_conv_w, hy_conv_b, kf, hy_bias)
    y = merge_heads((y_na, y_ml, y_sc, y_hy), out_gain)
    yc = None
    if need_ctx:
        kfc = hyena_filter_fft(pc.shape[1], hy_w1, hy_b1, hy_w2, hy_b2, hy_w3, hy_freq)
        yc_sc = short_conv_mixer(pc[..., o2:o3], sc_w)
        yc_hy = hyena_mixer(pc[..., o3:], hy_conv_w, hy_conv_b, kfc, hy_bias)
        yc = merge_heads((yc_na, yc_ml, yc_sc, yc_hy), out_gain)
    return y, yc


def moe_ffn(h, router_w, router_b, w_gate, w_up, w_down):
    B, L, D = h.shape
    T = B * L
    ht = h.reshape(T, D)
    scores = jax.nn.sigmoid(ht.astype(jnp.float32) @ router_w.astype(jnp.float32))
    sel = (scores + router_b.astype(jnp.float32)).reshape(T, N_GROUPS, EXPERTS_PER_GROUP)
    group_score = jnp.sum(lax.top_k(sel, 2)[0], axis=-1)
    g_idx = jnp.argmax(group_score, axis=-1)
    in_group = jnp.take_along_axis(sel, g_idx[:, None, None], axis=1)[:, 0]
    _, local = lax.top_k(in_group, TOP_K)
    expert = g_idx[:, None] * EXPERTS_PER_GROUP + local
    w_sel = jnp.take_along_axis(scores, expert, axis=1)
    gates = w_sel / jnp.sum(w_sel, axis=-1, keepdims=True)
    A = T * TOP_K
    flat_e = expert.reshape(A)
    order = jnp.argsort(flat_e)
    se = flat_e[order]
    counts = jnp.bincount(flat_e, length=N_EXPERTS)
    padded = (counts + MOE_BLOCK - 1) // MOE_BLOCK * MOE_BLOCK
    pend = jnp.cumsum(padded)
    pstart = pend - padded
    start = jnp.cumsum(counts) - counts
    dest = pstart[se] + jnp.arange(A) - start[se]
    n_rows = (A + MOE_BLOCK - 1) // MOE_BLOCK * MOE_BLOCK + N_EXPERTS * MOE_BLOCK
    row_tok = jnp.full((n_rows,), T, jnp.int32).at[dest].set((order // TOP_K).astype(jnp.int32))
    row_gate = jnp.zeros((n_rows,), h.dtype).at[dest].set(gates.reshape(A)[order].astype(h.dtype))
    n_blocks = n_rows // MOE_BLOCK
    block_e = jnp.minimum(jnp.searchsorted(pend, jnp.arange(n_blocks) * MOE_BLOCK, side='right'), N_EXPERTS - 1)
    h_pad = jnp.concatenate([ht, jnp.zeros((1, D), h.dtype)], axis=0)

    def expert_block(args):
        tok, e = args
        xb = h_pad[tok]
        return (jax.nn.silu(xb @ w_gate[e]) * (xb @ w_up[e])) @ w_down[e]

    yb = lax.map(expert_block, (row_tok.reshape(n_blocks, MOE_BLOCK), block_e))
    y = jax.ops.segment_sum(yb.reshape(n_rows, D) * row_gate[:, None], row_tok, num_segments=T + 1)[:T]
    return y.reshape(B, L, D)


def setup_inputs(seed: int = 0) -> dict:
    key = jax.random.key(seed)
    ks = jax.random.split(key, 34)
    f32 = jnp.float32
    D = D_MODEL

    def nrm(k, shape, s):
        return jax.random.normal(k, shape, f32) * s

    gate_i = nrm(ks[9], (DEPTH, 2, 1, ML_HEADS), 0.1)
    gate_f = jax.random.uniform(ks[10], (DEPTH, 2, 1, ML_HEADS), f32, 3.0, 6.0)
    return {
        'x': nrm(ks[0], (BATCH, SEQ, D), 1.0),
        'c': nrm(ks[1], (BATCH, D), 1.0),
        'ctx': nrm(ks[2], (BATCH, CTX_LEN, D), 1.0),
        'c_ctx': nrm(ks[3], (D,), 1.0),
        'w_ada': nrm(ks[4], (DEPTH, D, 6 * D), 0.5 * D ** -0.5),
        'b_ada': nrm(ks[5], (DEPTH, 6 * D), 0.02),
        'norm_mix': 1.0 + nrm(ks[6], (DEPTH, D), 0.02),
        'norm_ffn': 1.0 + nrm(ks[7], (DEPTH, D), 0.02),
        'w_in': nrm(ks[8], (DEPTH, D, D_IN), D ** -0.5),
        'mlstm_gate_bias': jnp.concatenate([gate_i, gate_f], axis=2).reshape(DEPTH, 4 * ML_HEADS),
        'w_out': nrm(ks[11], (DEPTH, D_MIX, D), D_MIX ** -0.5),
        'out_norm': 1.0 + nrm(ks[12], (DEPTH, D_MIX), 0.02),
        'na_q_norm': 1.0 + nrm(ks[13], (DEPTH, NA_HEAD_DIM), 0.02),
        'na_k_norm': 1.0 + nrm(ks[14], (DEPTH, NA_HEAD_DIM), 0.02),
        'na_rpb': nrm(ks[15], (DEPTH, NA_HEADS, 2 * NA_ROWS - 1, 2 * NA_COLS - 1), 0.1),
        'mlstm_conv_w': nrm(ks[16], (DEPTH, 3, 2 * ML_WIDTH), 3 ** -0.5),
        'mlstm_conv_b': nrm(ks[17], (DEPTH, 2 * ML_WIDTH), 0.02),
        'sconv_w': nrm(ks[18], (DEPTH, 3, SC_WIDTH), 3 ** -0.5),
        'hyena_conv_w': nrm(ks[19], (DEPTH, 3, 3 * HY_WIDTH), 3 ** -0.5),
        'hyena_conv_b': nrm(ks[20], (DEPTH, 3 * HY_WIDTH), 0.02),
        'hyena_f_w1': nrm(ks[21], (DEPTH, HY_EMB, HY_FFN), HY_EMB ** -0.5),
        'hyena_f_b1': nrm(ks[22], (DEPTH, HY_FFN), 0.02),
        'hyena_f_w2': nrm(ks[23], (DEPTH, HY_FFN, HY_FFN), HY_FFN ** -0.5),
        'hyena_f_b2': nrm(ks[24], (DEPTH, HY_FFN), 0.02),
        'hyena_f_w3': nrm(ks[25], (DEPTH, HY_FFN, HY_ORDER * 2 * HY_WIDTH), HY_FFN ** -0.5),
        'hyena_f_freq': 1.0 + nrm(ks[26], (DEPTH, 2, HY_FFN), 0.02),
        'hyena_bias': nrm(ks[27], (DEPTH, HY_ORDER, HY_WIDTH), 0.5),
        'router_w': nrm(ks[28], (D, N_EXPERTS), D ** -0.5),
        'router_bias': nrm(ks[29], (N_EXPERTS,), 0.01),
        'moe_w_gate': nrm(ks[30], (DEPTH, N_EXPERTS, D, EXPERT_FF), D ** -0.5),
        'moe_w_up': nrm(ks[31], (DEPTH, N_EXPERTS, D, EXPERT_FF), D ** -0.5),
        'moe_w_down': nrm(ks[32], (DEPTH, N_EXPERTS, EXPERT_FF, D), EXPERT_FF ** -0.5),
    }


def reference(x, c, ctx, c_ctx, w_ada, b_ada, norm_mix, norm_ffn, w_in, mlstm_gate_bias, w_out, out_norm, na_q_norm, na_k_norm, na_rpb, mlstm_conv_w, mlstm_conv_b, sconv_w, hyena_conv_w, hyena_conv_b, hyena_f_w1, hyena_f_b1, hyena_f_w2, hyena_f_b2, hyena_f_w3, hyena_f_freq, hyena_bias, router_w, router_bias, moe_w_gate, moe_w_up, moe_w_down):
    xc = ctx
    s_lat = jax.nn.silu(c)
    s_ctx = jax.nn.silu(c_ctx)
    for l in range(DEPTH):
        need_ctx = l < DEPTH - 1
        mod = jnp.split((s_lat @ w_ada[l] + b_ada[l])[:, None, :], 6, axis=-1)
        modc = jnp.split(s_ctx @ w_ada[l] + b_ada[l], 6, axis=-1)
        h = modulate(rms_norm(x, norm_mix[l]), mod[0], mod[1])
        hc = modulate(rms_norm(xc, norm_mix[l]), modc[0], modc[1])
        y, yc = hybrid_mixer(h @ w_in[l], hc @ w_in[l], na_q_norm[l], na_k_norm[l], na_rpb[l], mlstm_conv_w[l], mlstm_conv_b[l], mlstm_gate_bias[l], sconv_w[l], hyena_conv_w[l], hyena_conv_b[l], hyena_f_w1[l], hyena_f_b1[l], hyena_f_w2[l], hyena_f_b2[l], hyena_f_w3[l], hyena_f_freq[l], hyena_bias[l], out_norm[l], need_ctx)
        x = x + mod[2] * (y @ w_out[l])
        h2 = modulate(rms_norm(x, norm_ffn[l]), mod[3], mod[4])
        x = x + mod[5] * moe_ffn(h2, router_w, router_bias, moe_w_gate[l], moe_w_up[l], moe_w_down[l])
        if need_ctx:
            xc = xc + modc[2] * (yc @ w_out[l])
            h2c = modulate(rms_norm(xc, norm_ffn[l]), modc[3], modc[4])
            xc = xc + modc[5] * moe_ffn(h2c, router_w, router_bias, moe_w_gate[l], moe_w_up[l], moe_w_down[l])
    return x
```

```python
import functools
import math

import numpy as np
import jax
import jax.numpy as jnp
from jax import lax
from jax.experimental import pallas as pl
from jax.experimental.pallas import tpu as pltpu

F32 = jnp.float32
BF16 = jnp.bfloat16
U32 = jnp.uint32

D_MODEL = 2048
GRID_W = 64
NA_HEADS, NA_HD, NA_ROWS, NA_COLS = 8, 64, 8, 16
ML_HEADS, ML_HD = 4, 128
ML_GATES = 4 * ML_HEADS
ROPE_BASE = 10000.0
MIX_W = 512
HY_ORDER, HY_BANDS, HY_FFN = 2, 8, 64
HY_EMB = 1 + 2 * HY_BANDS
HY_FAST_DECAY, HY_SLOW_DECAY, HY_TARGET = 0.3, 1.5, 0.01
N_EXPERTS, N_GROUPS, TOP_K, EXPERT_FF = 32, 8, 2, 1408
EPG = N_EXPERTS // N_GROUPS
NORM_EPS = 1e-6
GATE_COL0 = 3 * MIX_W + 4 * MIX_W
P_COLS = 13 * MIX_W
CB_NA_Q, CB_NA_K, CB_NA_V = 0, 1, 2
CB_ML_Q, CB_ML_K, CB_ML_V, CB_ML_O = 3, 4, 5, 6
CB_SC_B, CB_SC_C, CB_SC_X = 7, 8, 9
CB_HY_V, CB_HY_X1, CB_HY_X2 = 10, 11, 12

LANES = 128
SUBLANES = 8
BF16_ROWS = 16
VMEM_LIMIT = 56 * 1024 * 1024

ML_CHUNK = 256
FFT_N2 = 256
FFT_JB = SUBLANES
MOE_BLOCK = 256
NEG = -1e30


def _cp(sem, vmem=VMEM_LIMIT):
    return pltpu.CompilerParams(dimension_semantics=sem, vmem_limit_bytes=vmem)


def _dot(a, b):
    return jnp.dot(a, b, preferred_element_type=F32)


def _dot_nt(a, b):
    return lax.dot_general(a, b, (((1,), (1,)), ((), ())), preferred_element_type=F32)


def _dot_hi(a, b):
    return jnp.dot(a, b, preferred_element_type=F32, precision=lax.Precision.HIGHEST)


def _pack(lo, hi):
    a = lax.bitcast_convert_type(lo.astype(BF16).astype(F32), U32) >> 16
    b = lax.bitcast_convert_type(hi.astype(BF16).astype(F32), U32) & jnp.uint32(0xFFFF0000)
    return a | b


def _unpack(w):
    lo = lax.bitcast_convert_type(w << 16, F32)
    hi = lax.bitcast_convert_type(w & jnp.uint32(0xFFFF0000), F32)
    return lo, hi


def _block_diag_mean(width, group):
    m = np.kron(np.eye(width // group), np.full((group, group), 1.0 / group))
    return jnp.asarray(m, BF16)


def _group_rms(y, bd, gain):
    ms = _dot((y * y).astype(BF16), bd)
    return y * lax.rsqrt(ms + NORM_EPS) * gain


def _ada_kernel(s_ref, w_ref, b_ref, o_ref):
    o_ref[...] = _dot_hi(s_ref[...], w_ref[...]) + b_ref[...]


def _ada(s, w, b):
    tn = 1536
    n = w.shape[1]
    return pl.pallas_call(
        _ada_kernel,
        grid=(n // tn,),
        in_specs=[pl.BlockSpec((SUBLANES, D_MODEL), lambda j: (0, 0)),
                  pl.BlockSpec((D_MODEL, tn), lambda j: (0, j)),
                  pl.BlockSpec((1, tn), lambda j: (0, j))],
        out_specs=pl.BlockSpec((SUBLANES, tn), lambda j: (0, j)),
        out_shape=jax.ShapeDtypeStruct((SUBLANES, n), F32),
        compiler_params=_cp(("arbitrary",)),
        name="ada",
    )(s, w, b.reshape(1, n))


def _inproj_kernel(x_ref, sh_ref, sc_ref, gain_ref, w_ref, wg_ref, bg_ref, p_ref, g_ref, h_scr):
    @pl.when(pl.program_id(2) == 0)
    def _():
        x = x_ref[0]
        ms = jnp.mean(x * x, axis=-1, keepdims=True)
        y = x * lax.rsqrt(ms + NORM_EPS) * gain_ref[...]
        h = (y * (1.0 + sc_ref[0]) + sh_ref[0]).astype(BF16)
        h_scr[...] = h
        g_ref[0] = _dot(h, wg_ref[...]) + bg_ref[...]

    p_ref[0] = _dot(h_scr[...], w_ref[...]).astype(BF16)


def _inproj(x, shift, scale, gain, w_main, w_gate, b_gate):
    B, L, D = x.shape
    tm = min(512, L)
    tn = P_COLS // 4
    return pl.pallas_call(
        _inproj_kernel,
        grid=(B, L // tm, P_COLS // tn),
        in_specs=[pl.BlockSpec((1, tm, D), lambda b, i, j: (b, i, 0)),
                  pl.BlockSpec((1, 1, D), lambda b, i, j: (b, 0, 0)),
                  pl.BlockSpec((1, 1, D), lambda b, i, j: (b, 0, 0)),
                  pl.BlockSpec((1, D), lambda b, i, j: (0, 0)),
                  pl.BlockSpec((D, tn), lambda b, i, j: (0, j)),
                  pl.BlockSpec((D, LANES), lambda b, i, j: (0, 0)),
                  pl.BlockSpec((1, LANES), lambda b, i, j: (0, 0))],
        out_specs=[pl.BlockSpec((1, tm, tn), lambda b, i, j: (b, i, j)),
                   pl.BlockSpec((1, tm, LANES), lambda b, i, j: (b, i, 0))],
        out_shape=[jax.ShapeDtypeStruct((B, L, P_COLS), BF16),
                   jax.ShapeDtypeStruct((B, L, LANES), F32)],
        scratch_shapes=[pltpu.VMEM((tm, D), BF16)],
        compiler_params=_cp(("arbitrary", "arbitrary", "arbitrary")),
        name="inproj",
    )(x, shift, scale, gain, w_main, w_gate, b_gate)


_CONV_BLOCKS = (CB_ML_Q, CB_ML_K, CB_SC_C, CB_SC_X, CB_HY_V, CB_HY_X1, CB_HY_X2)


def _prep_kernel(*refs, rope, tm):
    n_conv = len(_CONV_BLOCKS)
    main = refs[0:n_conv]
    prev = refs[n_conv:2 * n_conv]
    nxt = refs[2 * n_conv:3 * n_conv]
    k = 3 * n_conv
    scb_ref, mlw_ref, mlb_ref, scw_ref, hyw_ref, hyb_ref = refs[k:k + 6]
    k += 6
    if rope:
        cos_ref, sin_ref = refs[k:k + 2]
        k += 2
    q_ref, k_ref, sc_ref, hv_ref, hx1_ref, hx2_ref = refs[k:k + 6]

    i = pl.program_id(1)
    first = i == 0
    last = i == pl.num_programs(1) - 1
    rid = lax.broadcasted_iota(jnp.int32, (tm, MIX_W), 0)

    def rows(idx):
        xm = main[idx][0].astype(F32)
        xp = prev[idx][0].astype(F32)[BF16_ROWS - 1:BF16_ROWS, :]
        xn = nxt[idx][0].astype(F32)[0:1, :]
        return xm, xp, xn

    def conv(xm, xp, xn, w, b):
        xp = jnp.where(first, 0.0, xp)
        xn = jnp.where(last, 0.0, xn)
        up = jnp.where(rid == 0, xp, pltpu.roll(xm, 1, axis=0))
        dn = jnp.where(rid == tm - 1, xn, pltpu.roll(xm, tm - 1, axis=0))
        y = up * w[0:1] + xm * w[1:2] + dn * w[2:3]
        return y if b is None else y + b

    def silu(v):
        return v * jax.nn.sigmoid(v)

    def rot(t):
        if not rope:
            return t
        lane = lax.broadcasted_iota(jnp.int32, (tm, LANES), 1)
        first_half = (lane % 64) < 32
        cos, sin = cos_ref[...], sin_ref[...]
        outs = []
        for h in range(ML_HEADS):
            th = t[:, h * ML_HD:(h + 1) * ML_HD]
            partner = jnp.where(first_half, pltpu.roll(th, LANES - 32, axis=1), pltpu.roll(th, 32, axis=1))
            outs.append(th * cos + partner * sin)
        return jnp.concatenate(outs, axis=1)

    mlw, mlb = mlw_ref[...], mlb_ref[...]
    q = silu(conv(*rows(0), mlw[:, :MIX_W], mlb[:, :MIX_W]))
    kk = silu(conv(*rows(1), mlw[:, MIX_W:], mlb[:, MIX_W:]))
    q_ref[0] = rot(q).astype(BF16)
    k_ref[0] = (rot(kk) * ML_HD ** -0.5).astype(BF16)

    cm, cp_, cn = rows(2)
    xm, xp, xn = rows(3)
    sc = conv(cm * xm, cp_ * xp, cn * xn, scw_ref[...], None)
    sc_ref[0] = (scb_ref[0].astype(F32) * sc).astype(BF16)

    hyw, hyb = hyw_ref[...], hyb_ref[...]
    for n, o_ref in enumerate((hv_ref, hx1_ref, hx2_ref)):
        o_ref[0] = conv(*rows(4 + n), hyw[:, n * MIX_W:(n + 1) * MIX_W], hyb[:, n * MIX_W:(n + 1) * MIX_W])


def _prep(p, ml_w, ml_b, sc_w, hy_w, hy_b, rope_tabs):
    B, L, _ = p.shape
    tm = 256
    nt = L // tm
    hb = tm // BF16_ROWS
    n_halo = L // BF16_ROWS
    rope = rope_tabs is not None

    in_specs, args = [], []
    for cb in _CONV_BLOCKS:
        in_specs.append(pl.BlockSpec((1, tm, MIX_W), lambda b, i, cb=cb: (b, i, cb)))
        args.append(p)
    for cb in _CONV_BLOCKS:
        in_specs.append(pl.BlockSpec((1, BF16_ROWS, MIX_W), lambda b, i, cb=cb: (b, jnp.maximum(i * hb - 1, 0), cb)))
        args.append(p)
    for cb in _CONV_BLOCKS:
        in_specs.append(pl.BlockSpec((1, BF16_ROWS, MIX_W),
                                     lambda b, i, cb=cb: (b, jnp.minimum((i + 1) * hb, n_halo - 1), cb)))
        args.append(p)
    in_specs.append(pl.BlockSpec((1, tm, MIX_W), lambda b, i: (b, i, CB_SC_B)))
    args.append(p)
    for w in (ml_w, ml_b.reshape(1, -1), sc_w, hy_w, hy_b.reshape(1, -1)):
        in_specs.append(pl.BlockSpec(w.shape, lambda b, i: (0, 0)))
        args.append(w)
    if rope:
        for t in rope_tabs:
            in_specs.append(pl.BlockSpec((tm, LANES), lambda b, i: (i, 0)))
            args.append(t)
    blk = pl.BlockSpec((1, tm, MIX_W), lambda b, i: (b, i, 0))
    return pl.pallas_call(
        functools.partial(_prep_kernel, rope=rope, tm=tm),
        grid=(B, nt),
        in_specs=in_specs,
        out_specs=[blk] * 6,
        out_shape=[jax.ShapeDtypeStruct((B, L, MIX_W), BF16)] * 3 + [jax.ShapeDtypeStruct((B, L, MIX_W), F32)] * 3,
        compiler_params=_cp(("arbitrary", "arbitrary")),
        name="prep",
    )(*args)


def _rope_tables(L):
    quarter = ML_HD // 4
    pos = jnp.arange(L, dtype=jnp.int32)
    inv = ROPE_BASE ** (-jnp.arange(quarter, dtype=F32) / quarter)
    ar = (pos // GRID_W).astype(F32)[:, None] * inv[None, :]
    ac = (pos % GRID_W).astype(F32)[:, None] * inv[None, :]
    cos = jnp.concatenate([jnp.cos(ar), jnp.cos(ar), jnp.cos(ac), jnp.cos(ac)], axis=1)
    sin = jnp.concatenate([-jnp.sin(ar), jnp.sin(ar), -jnp.sin(ac), jnp.sin(ac)], axis=1)
    return cos, sin


NA_GROUP = 8
NA_Q = NA_GROUP * GRID_W
NA_KEYS = 2 * NA_Q


def _head_norm(x_bf16, bd, gain):
    x = x_bf16.astype(F32)
    return _group_rms(x, bd, gain)


def _pair_attention(qn, keys, vals, biases):
    lane = lax.broadcasted_iota(jnp.int32, qn.shape, 1)
    outs = []
    for half in range(2):
        sel = (lane < NA_HD) if half == 0 else (lane >= NA_HD)
        qh = jnp.where(sel, qn, jnp.zeros_like(qn))
        scores = []
        for kk, bias in zip(keys, biases[half]):
            s = _dot_nt(qh, kk)
            scores.append(s if bias is None else s + bias)
        m = functools.reduce(jnp.maximum, [jnp.max(s, axis=-1, keepdims=True) for s in scores])
        es = [jnp.exp(s - m) for s in scores]
        den = functools.reduce(jnp.add, [jnp.sum(e, axis=-1, keepdims=True) for e in es])
        o = functools.reduce(jnp.add, [_dot(e.astype(BF16), v) for e, v in zip(es, vals)])
        outs.append(o / den)
    return jnp.where(lane < NA_HD, outs[0], outs[1])


def _na_kernel(q_ref, kp_ref, kc_ref, kn_ref, vp_ref, vc_ref, vn_ref, ck_ref, cv_ref, tab_ref,
               qg_ref, kg_ref, bd_ref, o_ref):
    bd = bd_ref[...]
    half_q = NA_Q // 2
    qn = (_head_norm(q_ref[0], bd, qg_ref[...]) * NA_HD ** -0.5).astype(BF16)
    k_raw = jnp.concatenate([kp_ref[0][half_q:], kc_ref[0], kn_ref[0][:half_q]], axis=0)
    kun = _head_norm(k_raw, bd, kg_ref[...]).astype(BF16)
    vun = jnp.concatenate([vp_ref[0][half_q:], vc_ref[0], vn_ref[0][:half_q]], axis=0)
    ckn = _head_norm(ck_ref[0], bd, kg_ref[...]).astype(BF16)
    o = _pair_attention(qn, [kun, ckn], [vun, cv_ref[0]], [[tab_ref[0, 0], None], [tab_ref[0, 1], None]])
    o_ref[0] = o.astype(BF16)


def _na_table(rpb, rows):
    wr = NA_ROWS
    qc = np.arange(GRID_W)[:, None]
    kc = np.arange(GRID_W)[None, :]
    col0 = np.clip(qc - NA_COLS // 2, 0, GRID_W - NA_COLS)
    col_ok = (kc >= col0) & (kc < col0 + NA_COLS)
    dc = np.clip(kc - qc + NA_COLS - 1, 0, 2 * NA_COLS - 2)
    d_idx = np.zeros((3, NA_Q, NA_KEYS), np.int32)
    c_idx = np.zeros((3, NA_Q, NA_KEYS), np.int32)
    ok = np.zeros((3, NA_Q, NA_KEYS), bool)
    for kind, R in enumerate((0, NA_GROUP, rows - NA_GROUP)):
        for rr in range(NA_GROUP):
            r = R + rr
            r0 = int(np.clip(r - wr // 2, 0, rows - wr))
            for u in range(2 * NA_GROUP):
                key_row = R - NA_GROUP // 2 + u
                row_ok = r0 <= key_row <= r0 + wr - 1
                sl = (kind, slice(rr * GRID_W, (rr + 1) * GRID_W), slice(u * GRID_W, (u + 1) * GRID_W))
                d_idx[sl] = np.clip(key_row - r + wr - 1, 0, 2 * wr - 2)
                c_idx[sl] = dc
                ok[sl] = col_ok & row_ok
    tab = rpb[:, d_idx, c_idx]
    tab = jnp.where(ok[None], tab, NEG)
    return jnp.transpose(tab, (1, 0, 2, 3))


def _na(p, pc, rpb, q_gain, k_gain):
    B, L, _ = p.shape
    ctx = pc.shape[1]
    ng = L // NA_Q
    rows = L // GRID_W
    tab = _na_table(rpb.astype(F32), rows)
    qg = jnp.tile(q_gain.astype(F32), 2).reshape(1, LANES)
    kg = jnp.tile(k_gain.astype(F32), 2).reshape(1, LANES)
    bd = _block_diag_mean(LANES, NA_HD)
    npair = NA_HEADS // 2
    cpb = MIX_W // LANES

    def kind(i):
        return jnp.where(i == 0, 0, jnp.where(i == ng - 1, 2, 1))

    def blk(cb, which):
        def idx(hp, b, i):
            if which == 0:
                r = i
            elif which < 0:
                r = jnp.maximum(i - 1, 0)
            else:
                r = jnp.minimum(i + 1, ng - 1)
            return (b, r, cb * cpb + hp)
        return pl.BlockSpec((1, NA_Q, LANES), idx)

    def cblk(cb):
        return pl.BlockSpec((1, ctx, LANES), lambda hp, b, i: (b, 0, cb * cpb + hp))

    small = lambda shape: pl.BlockSpec(shape, lambda hp, b, i: (0, 0))
    return pl.pallas_call(
        _na_kernel,
        grid=(npair, B, ng),
        in_specs=[blk(CB_NA_Q, 0), blk(CB_NA_K, -1), blk(CB_NA_K, 0), blk(CB_NA_K, 1),
                  blk(CB_NA_V, -1), blk(CB_NA_V, 0), blk(CB_NA_V, 1), cblk(CB_NA_K), cblk(CB_NA_V),
                  pl.BlockSpec((1, 2, NA_Q, NA_KEYS), lambda hp, b, i: (kind(i), hp, 0, 0)),
                  small((1, LANES)), small((1, LANES)), small((LANES, LANES))],
        out_specs=pl.BlockSpec((1, NA_Q, LANES), lambda hp, b, i: (b, i, hp)),
        out_shape=jax.ShapeDtypeStruct((B, L, MIX_W), BF16),
        compiler_params=_cp(("arbitrary", "arbitrary", "arbitrary")),
        name="na",
    )(p, p, p, p, p, p, p, pc, pc, tab, qg, kg, bd)


def _ctx_attn_kernel(q_ref, k_ref, v_ref, qg_ref, kg_ref, bd_ref, o_ref):
    bd = bd_ref[...]
    qn = (_head_norm(q_ref[0], bd, qg_ref[...]) * NA_HD ** -0.5).astype(BF16)
    kn = _head_norm(k_ref[0], bd, kg_ref[...]).astype(BF16)
    o = _pair_attention(qn, [kn], [v_ref[0]], [[None], [None]])
    o_ref[0] = o.astype(BF16)


def _ctx_attn(pc, q_gain, k_gain):
    B, ctx, _ = pc.shape
    qg = jnp.tile(q_gain.astype(F32), 2).reshape(1, LANES)
    kg = jnp.tile(k_gain.astype(F32), 2).reshape(1, LANES)
    bd = _block_diag_mean(LANES, NA_HD)
    cpb = MIX_W // LANES
    blk = lambda cb: pl.BlockSpec((1, ctx, LANES), lambda hp, b: (b, 0, cb * cpb + hp))
    small = lambda shape: pl.BlockSpec(shape, lambda hp, b: (0, 0))
    return pl.pallas_call(
        _ctx_attn_kernel,
        grid=(NA_HEADS // 2, B),
        in_specs=[blk(CB_NA_Q), blk(CB_NA_K), blk(CB_NA_V), small((1, LANES)), small((1, LANES)),
                  small((LANES, LANES))],
        out_specs=pl.BlockSpec((1, ctx, LANES), lambda hp, b: (b, 0, hp)),
        out_shape=jax.ShapeDtypeStruct((B, ctx, MIX_W), BF16),
        compiler_params=_cp(("arbitrary", "arbitrary")),
        name="ctx_attn",
    )(pc, pc, pc, qg, kg, bd)


def _log_sigmoid(x):
    return jnp.minimum(x, 0.0) - jnp.log(1.0 + jnp.exp(-jnp.abs(x)))


def _split_dot(a, b_f32, a_is_const):
    if a_is_const:
        hi = b_f32.astype(BF16)
        lo = (b_f32 - hi.astype(F32)).astype(BF16)
        return _dot(a, hi) + _dot(a, lo)
    hi = a.astype(BF16)
    lo = (a - hi.astype(F32)).astype(BF16)
    return _dot(hi, b_f32) + _dot(lo, b_f32)


def _mlstm_kernel(*refs):
    (qf, kf, vf, gf, gtf, qb, kb, vb, gb, gtb, qc, kc, vc, gc, gtc, tri_ref, trit_ref,
     hf_ref, hb_ref, hcf_ref, hcb_ref, c_scr, m_scr) = refs
    cl = ML_CHUNK
    s = pl.program_id(1)

    @pl.when(s == 0)
    def _():
        c_scr[...] = jnp.zeros_like(c_scr)
        m_scr[...] = jnp.zeros_like(m_scr)

    tri, trit = tri_ref[...], trit_ref[...]
    row_i = lax.broadcasted_iota(jnp.int32, (cl, cl), 0)
    col_i = lax.broadcasted_iota(jnp.int32, (cl, cl), 1)
    lane = lax.broadcasted_iota(jnp.int32, (cl, LANES), 1)
    ones_col = jnp.where(lane == 0, 1.0, 0.0).astype(BF16)

    def stream(d, q_ref, k_ref, v_ref, g_ref, gt_ref, out_ref):
        g = g_ref[0]
        gt = gt_ref[0]
        lf_cols = _log_sigmoid(g)
        lf_rows = _log_sigmoid(gt)
        if d == 0:
            b_cols = _split_dot(tri, lf_cols, True)
            b_rows = _split_dot(lf_rows, trit, False)
            mask = col_i <= row_i
        else:
            b_cols = _split_dot(trit, lf_cols, True)
            b_rows = _split_dot(lf_rows, tri, False)
            mask = col_i >= row_i
        outs = []
        for h in range(ML_HEADS):
            ci = (2 * d) * ML_HEADS + h
            cf = (2 * d + 1) * ML_HEADS + h
            q = q_ref[0][:, h * ML_HD:(h + 1) * ML_HD]
            k = k_ref[0][:, h * ML_HD:(h + 1) * ML_HD]
            v_aug = jnp.concatenate([v_ref[0][:, h * ML_HD:(h + 1) * ML_HD], ones_col], axis=1)
            b_col = b_cols[:, cf:cf + 1]
            b_row = b_rows[cf:cf + 1, :]
            li_row = gt[ci:ci + 1, :]
            m_prev = m_scr[d, h][0:1, 0:1]
            ct = c_scr[d, h]
            dm = jnp.where(mask, b_col - b_row + li_row, -jnp.inf)
            inter = b_col + m_prev
            m_t = jnp.maximum(jnp.max(dm, axis=-1, keepdims=True), inter)
            sc = _dot_nt(q, k) * jnp.exp(dm - m_t)
            gi = jnp.exp(inter - m_t)
            nd = _dot(sc.astype(BF16), v_aug) + gi * _dot(q, ct.astype(BF16))
            num = nd[:, :ML_HD]
            den = nd[:, ML_HD:ML_HD + 1]
            outs.append(num / jnp.maximum(jnp.abs(den), jnp.exp(-m_t)))
            b_last = b_row[:, cl - 1:cl] if d == 0 else b_row[:, 0:1]
            tail = b_last - b_row + li_row
            m_new = jnp.maximum(b_last + m_prev, jnp.max(tail, axis=-1, keepdims=True))
            w_row = jnp.exp(tail - m_new)
            decay = jnp.exp(b_last + m_prev - m_new)
            kt = k.astype(F32).T
            c_scr[d, h] = decay * ct + _dot((kt * w_row).astype(BF16), v_aug)
            m_scr[d, h] = jnp.broadcast_to(m_new, (SUBLANES, LANES))
        out_ref[0] = jnp.concatenate(outs, axis=1).astype(BF16)

    @pl.when(s == 0)
    def _():
        stream(0, qc, kc, vc, gc, gtc, hcf_ref)
        stream(1, qc, kc, vc, gc, gtc, hcb_ref)

    @pl.when(s > 0)
    def _():
        stream(0, qf, kf, vf, gf, gtf, hf_ref)
        stream(1, qb, kb, vb, gb, gtb, hb_ref)


def _mlstm(mlq, mlk, p, g, mlqc, mlkc, pc, gc):
    B, L, _ = mlq.shape
    cl = ML_CHUNK
    assert pc.shape[1] == cl
    n = L // cl
    gt = jnp.swapaxes(g[..., :ML_GATES], 1, 2)
    gtc = jnp.swapaxes(gc[..., :ML_GATES], 1, 2)
    tri = jnp.asarray(np.tril(np.ones((cl, cl))), BF16)
    trit = jnp.asarray(np.triu(np.ones((cl, cl))), BF16)

    fwd = lambda s: jnp.maximum(s - 1, 0)
    bwd = lambda s: jnp.clip(n - s, 0, n - 1)

    def specs(sel, vcol):
        return [pl.BlockSpec((1, cl, MIX_W), lambda b, s: (b, sel(s), 0)),
                pl.BlockSpec((1, cl, MIX_W), lambda b, s: (b, sel(s), 0)),
                pl.BlockSpec((1, cl, MIX_W), lambda b, s: (b, sel(s), vcol)),
                pl.BlockSpec((1, cl, LANES), lambda b, s: (b, sel(s), 0)),
                pl.BlockSpec((1, ML_GATES, cl), lambda b, s: (b, 0, sel(s)))]

    zero = lambda s: 0
    const = pl.BlockSpec((cl, cl), lambda b, s: (0, 0))
    return pl.pallas_call(
        _mlstm_kernel,
        grid=(B, n + 1),
        in_specs=specs(fwd, CB_ML_V) + specs(bwd, CB_ML_V) + specs(zero, CB_ML_V) + [const, const],
        out_specs=[pl.BlockSpec((1, cl, MIX_W), lambda b, s: (b, fwd(s), 0)),
                   pl.BlockSpec((1, cl, MIX_W), lambda b, s: (b, bwd(s), 0)),
                   pl.BlockSpec((1, cl, MIX_W), lambda b, s: (b, 0, 0)),
                   pl.BlockSpec((1, cl, MIX_W), lambda b, s: (b, 0, 0))],
        out_shape=[jax.ShapeDtypeStruct((B, L, MIX_W), BF16)] * 2 + [jax.ShapeDtypeStruct((B, cl, MIX_W), BF16)] * 2,
        scratch_shapes=[pltpu.VMEM((2, ML_HEADS, ML_HD, 2 * ML_HD), F32),
                        pltpu.VMEM((2, ML_HEADS, SUBLANES, LANES), F32)],
        compiler_params=_cp(("arbitrary", "arbitrary")),
        name="mlstm",
    )(mlq, mlk, p, g, gt, mlq, mlk, p, g, gt, mlqc, mlkc, pc, gc, gtc, tri, trit)


def _filter_kernel(z_ref, w1_ref, b1_ref, w2_ref, b2_ref, w3_ref, fr_ref, dl_ref, j_ref, hf_ref, hb_ref, *, length):
    z = z_ref[...]
    fr = fr_ref[...]
    h1 = jnp.sin(fr[0:1] * (_dot_hi(z, w1_ref[...]) + b1_ref[...]))
    h2 = jnp.sin(fr[1:2] * (_dot_hi(h1, w2_ref[...]) + b2_ref[...]))
    h = _dot_hi(h2, w3_ref[...])
    t = z[:, 0:1]
    decay = jnp.exp(-t * dl_ref[...])
    fwd, bwd = [], []
    for o in range(HY_ORDER):
        fwd.append(h[:, (2 * o) * MIX_W:(2 * o + 1) * MIX_W] * decay)
        bwd.append(h[:, (2 * o + 1) * MIX_W:(2 * o + 2) * MIX_W] * decay)
    hf_ref[...] = jnp.concatenate(fwd, axis=1)
    hb = jnp.concatenate(bwd, axis=1)
    jm = j_ref[...]
    a = hb.astype(BF16)
    r1 = hb - a.astype(F32)
    b = r1.astype(BF16)
    c = (r1 - b.astype(F32)).astype(BF16)
    hb_ref[...] = _dot(jm, a) + _dot(jm, b) + _dot(jm, c)


def _hyena_filter(L, w1, b1, w2, b2, w3, freq):
    tb = FFT_N2
    nb = L // tb
    t = (jnp.arange(L, dtype=F32) / L)[:, None]
    bands = jnp.linspace(1e-4, HY_BANDS - 1, HY_BANDS, dtype=F32)
    ang = 2 * math.pi * t * bands
    z = jnp.concatenate([t, jnp.cos(ang), jnp.sin(ang), jnp.zeros((L, LANES - HY_EMB), F32)], axis=-1)
    deltas = np.abs(np.linspace(math.log(HY_TARGET) / HY_SLOW_DECAY, math.log(HY_TARGET) / HY_FAST_DECAY, MIX_W,
                                dtype=np.float32)).reshape(1, MIX_W)
    w1p = jnp.zeros((LANES, HY_FFN), F32).at[:HY_EMB].set(w1.astype(F32))
    jm = jnp.asarray(np.eye(tb)[::-1].copy(), BF16)
    width = HY_ORDER * MIX_W
    full = lambda a: pl.BlockSpec(a.shape, lambda i: (0,) * a.ndim)
    args = (z, w1p, b1.reshape(1, -1).astype(F32), w2.astype(F32), b2.reshape(1, -1).astype(F32),
            w3.astype(F32), freq.astype(F32), jnp.asarray(deltas), jm)
    hf, hb_rev = pl.pallas_call(
        functools.partial(_filter_kernel, length=L),
        grid=(nb,),
        in_specs=[pl.BlockSpec((tb, LANES), lambda i: (i, 0))] + [full(a) for a in args[1:]],
        out_specs=[pl.BlockSpec((tb, width), lambda i: (i, 0)),
                   pl.BlockSpec((tb, width), lambda i: (nb - 1 - i, 0))],
        out_shape=[jax.ShapeDtypeStruct((L, width), F32)] * 2,
        compiler_params=_cp(("arbitrary",)),
        name="hy_filter",
    )(*args)
    return hf, hb_rev


def _fft_consts(n1, n1v):
    jb = FFT_JB
    th1 = 2 * np.pi * np.outer(np.arange(n1), np.arange(n1v)) / n1
    f1 = np.kron(np.concatenate([np.cos(th1), -np.sin(th1)], axis=0), np.eye(jb))
    i1 = np.kron(np.concatenate([np.cos(th1.T), -np.sin(th1.T)], axis=1), np.eye(jb))
    th2 = 2 * np.pi * np.outer(np.arange(FFT_N2), np.arange(FFT_N2)) / FFT_N2
    c2, s2 = np.cos(th2), np.sin(th2)
    f2 = np.block([[c2, s2], [-s2, c2]])
    i2 = np.block([[c2, -s2], [s2, c2]])
    prod = (jnp.arange(n1, dtype=jnp.int32)[:, None] * jnp.arange(FFT_N2, dtype=jnp.int32)[None, :]).astype(F32)
    tht = prod * (2 * math.pi / (n1 * FFT_N2))
    twr = jnp.broadcast_to(jnp.cos(tht)[:, :, None], (n1, FFT_N2, LANES))
    twi = jnp.broadcast_to(-jnp.sin(tht)[:, :, None], (n1, FFT_N2, LANES))
    return (jnp.asarray(f1, BF16), jnp.asarray(i1, BF16), jnp.asarray(f2, BF16), jnp.asarray(i2, BF16), twr, twi)


def _fft_outer_kernel(u_ref, f1_ref, a_ref):
    n1v, jb, ct = u_ref.shape[1:]
    n1 = a_ref.shape[1]
    u = u_ref[0].reshape(n1v * jb, ct).astype(BF16)
    res = _dot(f1_ref[...], u)
    half = n1 * jb
    a_ref[0] = _pack(res[:half], res[half:]).reshape(n1, jb, ct)


def _fft_outer(u4, f1, n1):
    B, n1v, _, C = u4.shape
    ct = MIX_W
    return pl.pallas_call(
        _fft_outer_kernel,
        grid=(B, FFT_N2 // FFT_JB, C // ct),
        in_specs=[pl.BlockSpec((1, n1v, FFT_JB, ct), lambda b, j, c: (b, 0, j, c)),
                  pl.BlockSpec(f1.shape, lambda b, j, c: (0, 0))],
        out_specs=pl.BlockSpec((1, n1, FFT_JB, ct), lambda b, j, c: (b, 0, j, c)),
        out_shape=jax.ShapeDtypeStruct((B, n1, FFT_N2, C), U32),
        compiler_params=_cp(("arbitrary", "arbitrary", "arbitrary")),
        name="fft_outer",
    )(u4, f1)


def _twiddle(re, im, twr, twi, conj):
    reps = re.shape[1] // LANES
    tr = jnp.concatenate([twr] * reps, axis=1)
    ti = jnp.concatenate([twi] * reps, axis=1)
    if conj:
        ti = -ti
    return re * tr - im * ti, re * ti + im * tr


def _filter_spec_kernel(a_ref, twr_ref, twi_ref, f2_ref, kf_ref):
    re, im = _unpack(a_ref[0, 0])
    re, im = _twiddle(re, im, twr_ref[0], twi_ref[0], False)
    x = _dot(f2_ref[...], jnp.concatenate([re, im], axis=0).astype(BF16))
    kf_ref[0] = _pack(x[:FFT_N2], x[FFT_N2:])


def _filter_spec(a, twr, twi, f2):
    _, n1, _, C = a.shape
    return pl.pallas_call(
        _filter_spec_kernel,
        grid=(n1,),
        in_specs=[pl.BlockSpec((1, 1, FFT_N2, C), lambda k: (0, k, 0, 0)),
                  pl.BlockSpec((1, FFT_N2, LANES), lambda k: (k, 0, 0)),
                  pl.BlockSpec((1, FFT_N2, LANES), lambda k: (k, 0, 0)),
                  pl.BlockSpec(f2.shape, lambda k: (0, 0))],
        out_specs=pl.BlockSpec((1, FFT_N2, C), lambda k: (k, 0, 0)),
        out_shape=jax.ShapeDtypeStruct((n1, FFT_N2, C), U32),
        compiler_params=_cp(("arbitrary",)),
        name="filter_spec",
    )(a, twr, twi, f2)


def _fft_mid_kernel(a_ref, kf_ref, twr_ref, twi_ref, f2_ref, i2_ref, d_ref):
    nb = a_ref.shape[0]
    twr, twi = twr_ref[0], twi_ref[0]
    res, ims = [], []
    for b in range(nb):
        re, im = _unpack(a_ref[b, 0])
        re, im = _twiddle(re, im, twr, twi, False)
        res.append(re)
        ims.append(im)
    stacked = jnp.concatenate([jnp.concatenate(res, axis=1), jnp.concatenate(ims, axis=1)], axis=0)
    x = _dot(f2_ref[...], stacked.astype(BF16))
    xr, xi = x[:FFT_N2], x[FFT_N2:]
    kr, ki = _unpack(kf_ref[0])
    kr = jnp.concatenate([kr] * nb, axis=1)
    ki = jnp.concatenate([ki] * nb, axis=1)
    zr = xr * kr - xi * ki
    zi = xr * ki + xi * kr
    y = _dot(i2_ref[...], jnp.concatenate([zr, zi], axis=0).astype(BF16))
    yr, yi = _twiddle(y[:FFT_N2], y[FFT_N2:], twr, twi, True)
    c = a_ref.shape[-1]
    for b in range(nb):
        d_ref[b, 0] = _pack(yr[:, b * c:(b + 1) * c], yi[:, b * c:(b + 1) * c])


def _fft_mid(a, kf, order, twr, twi, f2, i2):
    B, n1, _, C = a.shape
    return pl.pallas_call(
        _fft_mid_kernel,
        grid=(n1,),
        in_specs=[pl.BlockSpec((B, 1, FFT_N2, C), lambda k: (0, k, 0, 0)),
                  pl.BlockSpec((1, FFT_N2, C), lambda k: (k, 0, order)),
                  pl.BlockSpec((1, FFT_N2, LANES), lambda k: (k, 0, 0)),
                  pl.BlockSpec((1, FFT_N2, LANES), lambda k: (k, 0, 0)),
                  pl.BlockSpec(f2.shape, lambda k: (0, 0)),
                  pl.BlockSpec(i2.shape, lambda k: (0, 0))],
        out_specs=pl.BlockSpec((B, 1, FFT_N2, C), lambda k: (0, k, 0, 0)),
        out_shape=jax.ShapeDtypeStruct((B, n1, FFT_N2, C), U32),
        compiler_params=_cp(("arbitrary",)),
        name="fft_mid",
    )(a, kf, twr, twi, f2, i2)


def _fft_inv_outer_kernel(d_ref, i1_ref, u_ref, x_ref, bias_ref, y_ref, *, scale):
    n1, jb, ct = d_ref.shape[1:]
    n1v = y_ref.shape[1]
    re, im = _unpack(d_ref[0].reshape(n1 * jb, ct))
    conv = _dot(i1_ref[...], jnp.concatenate([re, im], axis=0).astype(BF16)) * scale
    u = u_ref[0].reshape(n1v * jb, ct)
    y = x_ref[0].reshape(n1v * jb, ct) * (conv + bias_ref[...] * u)
    y_ref[0] = y.reshape(n1v, jb, ct)


def _fft_inv_outer(d, i1, u4, x4, bias, n_fft):
    B, n1, _, C = d.shape
    n1v = u4.shape[1]
    ct = MIX_W
    sig = pl.BlockSpec((1, n1v, FFT_JB, ct), lambda b, j, c: (b, 0, j, c))
    return pl.pallas_call(
        functools.partial(_fft_inv_outer_kernel, scale=1.0 / n_fft),
        grid=(B, FFT_N2 // FFT_JB, C // ct),
        in_specs=[pl.BlockSpec((1, n1, FFT_JB, ct), lambda b, j, c: (b, 0, j, c)),
                  pl.BlockSpec(i1.shape, lambda b, j, c: (0, 0)),
                  sig, sig,
                  pl.BlockSpec((1, ct), lambda b, j, c: (0, c))],
        out_specs=sig,
        out_shape=jax.ShapeDtypeStruct(u4.shape, F32),
        compiler_params=_cp(("arbitrary", "arbitrary", "arbitrary")),
        name="fft_inv_outer",
    )(d, i1, u4, x4, bias)


def _hyena(hv, hx1, hx2, w1, b1, w2, b2, w3, freq, bias):
    B, L, C = hv.shape
    lp = max(L, 8 * FFT_N2)
    n_fft = 2 * lp
    n1 = n_fft // FFT_N2
    n1v = lp // FFT_N2
    f1s, i1s, f2, i2, twr, twi = _fft_consts(n1, n1v)
    f1full = _fft_consts(n1, n1)[0]

    hf, hb_rev = _hyena_filter(L, w1, b1, w2, b2, w3, freq)
    width = HY_ORDER * C
    kern = jnp.concatenate([hf, jnp.zeros((n_fft - 2 * L + 1, width), F32), hb_rev[:L - 1]], axis=0)
    ka = _fft_outer(kern.reshape(1, n1, FFT_N2, width), f1full, n1)
    kf = _filter_spec(ka, twr, twi, f2)

    def pad(t):
        if lp != L:
            t = jnp.concatenate([t, jnp.zeros((B, lp - L, C), t.dtype)], axis=1)
        return t.reshape(B, n1v, FFT_N2, C)

    y = pad(hv)
    for o, xg in enumerate((hx1, hx2)):
        a = _fft_outer(y, f1s, n1)
        d = _fft_mid(a, kf, o, twr, twi, f2, i2)
        y = _fft_inv_outer(d, i1s, y, pad(xg), bias[o].reshape(1, C).astype(F32), n_fft)
    return y.reshape(B, lp, C)[:, :L]


def _merge_kernel(na_ref, hf_ref, hb_ref, og_ref, sc_ref, hy_ref, gain_ref, wout_ref, x_ref, m2_ref, m3_ref, m4_ref,
                  gffn_ref, rwh_ref, rwl_ref, bd64_ref, bd128_ref, xo_ref, hp_ref, s_ref):
    gain = gain_ref[...]
    y_ml = jax.nn.sigmoid(og_ref[0].astype(F32)) * (hf_ref[0].astype(F32) + hb_ref[0].astype(F32))
    parts = ((na_ref[0].astype(F32), bd64_ref), (y_ml, bd128_ref), (sc_ref[0].astype(F32), bd64_ref),
             (hy_ref[0], bd64_ref))
    acc = None
    for n, (y, bd) in enumerate(parts):
        yn = _group_rms(y, bd[...], gain[:, n * MIX_W:(n + 1) * MIX_W]).astype(BF16)
        t = _dot(yn, wout_ref[n * MIX_W:(n + 1) * MIX_W, :])
        acc = t if acc is None else acc + t
    x = x_ref[0] + m2_ref[0] * acc
    xo_ref[0] = x
    ms = jnp.mean(x * x, axis=-1, keepdims=True)
    h2 = (x * lax.rsqrt(ms + NORM_EPS) * gffn_ref[...]) * (1.0 + m4_ref[0]) + m3_ref[0]
    hi = h2.astype(BF16)
    lo = (h2 - hi.astype(F32)).astype(BF16)
    logits = _dot(hi, rwh_ref[...]) + _dot(hi, rwl_ref[...]) + _dot(lo, rwh_ref[...])
    s_ref[0] = jax.nn.sigmoid(logits)
    half = D_MODEL // 2
    hf32 = hi.astype(F32)
    hp_ref[0] = _pack(hf32[:, :half], hf32[:, half:])


def _merge(y_na, hf, hb, p, y_sc, y_hy, out_gain, w_out, x, m2, m3, m4, g_ffn, rwh, rwl):
    B, L, D = x.shape
    tm = min(256, L)
    bd64 = _block_diag_mean(MIX_W, 64)
    bd128 = _block_diag_mean(MIX_W, 128)
    row = lambda w: pl.BlockSpec((1, tm, w), lambda b, i: (b, i, 0))
    mod = pl.BlockSpec((1, 1, D), lambda b, i: (b, 0, 0))
    full = lambda a: pl.BlockSpec(a.shape, lambda b, i: (0,) * a.ndim)
    return pl.pallas_call(
        _merge_kernel,
        grid=(B, L // tm),
        in_specs=[row(MIX_W), row(MIX_W), row(MIX_W),
                  pl.BlockSpec((1, tm, MIX_W), lambda b, i: (b, i, CB_ML_O)),
                  row(MIX_W), row(MIX_W), full(out_gain), full(w_out), row(D), mod, mod, mod,
                  full(g_ffn), full(rwh), full(rwl), full(bd64), full(bd128)],
        out_specs=[row(D), row(D // 2), row(LANES)],
        out_shape=[jax.ShapeDtypeStruct((B, L, D), F32), jax.ShapeDtypeStruct((B, L, D // 2), U32),
                   jax.ShapeDtypeStruct((B, L, LANES), F32)],
        compiler_params=_cp(("arbitrary", "arbitrary")),
        name="merge",
    )(y_na, hf, hb, p, y_sc, y_hy, out_gain, w_out, x, m2, m3, m4, g_ffn, rwh, rwl, bd64, bd128)


def _route(scores, router_b):
    T = scores.shape[0]
    sel = (scores + router_b.astype(F32)).reshape(T, N_GROUPS, EPG)
    group_score = jnp.sum(lax.top_k(sel, 2)[0], axis=-1)
    g_idx = jnp.argmax(group_score, axis=-1)
    in_group = jnp.take_along_axis(sel, g_idx[:, None, None], axis=1)[:, 0]
    _, local = lax.top_k(in_group, TOP_K)
    expert = (g_idx[:, None] * EPG + local).astype(jnp.int32)
    w_sel = jnp.take_along_axis(scores, expert, axis=1)
    gates = w_sel / jnp.sum(w_sel, axis=-1, keepdims=True)
    return expert, gates


def _plan(expert):
    T = expert.shape[0]
    A = T * TOP_K
    flat = expert.reshape(A)
    onehot = (flat[:, None] == jnp.arange(N_EXPERTS, dtype=jnp.int32)[None, :]).astype(jnp.int32)
    csum = jnp.cumsum(onehot, axis=0)
    rank = jnp.take_along_axis(csum, flat[:, None], axis=1)[:, 0] - 1
    counts = csum[-1]
    padded = (counts + MOE_BLOCK - 1) // MOE_BLOCK * MOE_BLOCK
    pend = jnp.cumsum(padded)
    pstart = pend - padded
    dest = (pstart[flat] + rank).astype(jnp.int32)
    n_rows = (A + MOE_BLOCK - 1) // MOE_BLOCK * MOE_BLOCK + N_EXPERTS * MOE_BLOCK
    n_blocks = n_rows // MOE_BLOCK
    block_e = jnp.minimum(jnp.searchsorted(pend, jnp.arange(n_blocks, dtype=jnp.int32) * MOE_BLOCK, side='right'),
                          N_EXPERTS - 1).astype(jnp.int32)
    n_used = (pend[-1] // MOE_BLOCK).astype(jnp.int32).reshape(1)
    return dest.reshape(T, TOP_K), block_e, n_used, n_rows


def _scatter_kernel(dest_ref, h_ref, xs_in_ref, xs_ref, sem, *, tm):
    del xs_in_ref

    def copy(t, d):
        return pltpu.make_async_copy(h_ref.at[pl.ds(t, 1), :], xs_ref.at[pl.ds(d, 1), :], sem)

    def issue(t, c):
        for k in range(TOP_K):
            copy(t, dest_ref[0, 0, TOP_K * t + k]).start()
        return c

    def drain(t, c):
        for k in range(TOP_K):
            copy(t, dest_ref[0, 0, TOP_K * t + k]).wait()
        return c

    lax.fori_loop(0, tm, issue, 0)
    lax.fori_loop(0, tm, drain, 0)


def _scatter(h2p, dest, xs):
    T, W = h2p.shape
    tm = 256
    return pl.pallas_call(
        functools.partial(_scatter_kernel, tm=tm),
        grid=(T // tm,),
        in_specs=[pl.BlockSpec((1, 1, TOP_K * tm), lambda i: (i, 0, 0), memory_space=pltpu.SMEM),
                  pl.BlockSpec((tm, W), lambda i: (i, 0)),
                  pl.BlockSpec(memory_space=pl.ANY)],
        out_specs=pl.BlockSpec(memory_space=pl.ANY),
        out_shape=jax.ShapeDtypeStruct(xs.shape, xs.dtype),
        scratch_shapes=[pltpu.SemaphoreType.DMA(())],
        input_output_aliases={2: 0},
        compiler_params=_cp(("arbitrary",)),
        name="moe_scatter",
    )(dest.reshape(T // tm, 1, TOP_K * tm), h2p, xs)


def _ffn_kernel(be_ref, nu_ref, x_ref, wg_ref, wu_ref, wd_ref, y_ref):
    del be_ref
    used = pl.program_id(0) < nu_ref[0]
    half = D_MODEL // 2

    @pl.when(used)
    def _():
        lo, hi = _unpack(x_ref[...])
        lo, hi = lo.astype(BF16), hi.astype(BF16)
        g = _dot(lo, wg_ref[0, :half, :]) + _dot(hi, wg_ref[0, half:, :])
        u = _dot(lo, wu_ref[0, :half, :]) + _dot(hi, wu_ref[0, half:, :])
        h = (g * jax.nn.sigmoid(g) * u).astype(BF16)
        y = _dot(h, wd_ref[0])
        y_ref[...] = _pack(y[:, :half], y[:, half:])

    @pl.when(jnp.logical_not(used))
    def _():
        y_ref[...] = jnp.zeros_like(y_ref)


def _ffn(xs, block_e, n_used, wg, wu, wd):
    n_rows, W = xs.shape
    nb = n_rows // MOE_BLOCK
    grid_spec = pltpu.PrefetchScalarGridSpec(
        num_scalar_prefetch=2,
        grid=(nb,),
        in_specs=[pl.BlockSpec((MOE_BLOCK, W), lambda i, be, nu: (i, 0)),
                  pl.BlockSpec((1, D_MODEL, EXPERT_FF), lambda i, be, nu: (be[i], 0, 0)),
                  pl.BlockSpec((1, D_MODEL, EXPERT_FF), lambda i, be, nu: (be[i], 0, 0)),
                  pl.BlockSpec((1, EXPERT_FF, D_MODEL), lambda i, be, nu: (be[i], 0, 0))],
        out_specs=pl.BlockSpec((MOE_BLOCK, W), lambda i, be, nu: (i, 0)),
    )
    return pl.pallas_call(
        _ffn_kernel,
        grid_spec=grid_spec,
        out_shape=jax.ShapeDtypeStruct((n_rows, W), U32),
        compiler_params=_cp(("arbitrary",)),
        name="moe_ffn",
    )(block_e, n_used, xs, wg, wu, wd)


def _combine_kernel(dest_ref, yb_ref, g_ref, x_ref, m5_ref, o_ref, buf, sem, *, tm):
    def copy(t, k, d):
        return pltpu.make_async_copy(yb_ref.at[pl.ds(d, 1), :], buf.at[k, pl.ds(t, 1), :], sem)

    def issue(t, c):
        for k in range(TOP_K):
            copy(t, k, dest_ref[0, 0, TOP_K * t + k]).start()
        return c

    def drain(t, c):
        for k in range(TOP_K):
            copy(t, k, dest_ref[0, 0, TOP_K * t + k]).wait()
        return c

    lax.fori_loop(0, tm, issue, 0)
    lax.fori_loop(0, tm, drain, 0)
    g = g_ref[...]
    a_lo, a_hi = _unpack(buf[0])
    b_lo, b_hi = _unpack(buf[1])
    g0, g1 = g[:, 0:1], g[:, 1:2]
    half = D_MODEL // 2
    m5 = m5_ref[0]
    o_ref[:, :half] = x_ref[:, :half] + m5[:, :half] * (g0 * a_lo + g1 * b_lo)
    o_ref[:, half:] = x_ref[:, half:] + m5[:, half:] * (g0 * a_hi + g1 * b_hi)


def _combine(yb, dest, gates, x2d, m5, rows_per_batch):
    T, D = x2d.shape
    tm = min(256, rows_per_batch)
    tpb = rows_per_batch // tm
    gp = jnp.zeros((T, LANES), F32).at[:, :TOP_K].set(gates.astype(F32))
    return pl.pallas_call(
        functools.partial(_combine_kernel, tm=tm),
        grid=(T // tm,),
        in_specs=[pl.BlockSpec((1, 1, TOP_K * tm), lambda i: (i, 0, 0), memory_space=pltpu.SMEM),
                  pl.BlockSpec(memory_space=pl.ANY),
                  pl.BlockSpec((tm, LANES), lambda i: (i, 0)),
                  pl.BlockSpec((tm, D), lambda i: (i, 0)),
                  pl.BlockSpec((1, 1, D), lambda i: (i // tpb, 0, 0))],
        out_specs=pl.BlockSpec((tm, D), lambda i: (i, 0)),
        out_shape=jax.ShapeDtypeStruct((T, D), F32),
        scratch_shapes=[pltpu.VMEM((TOP_K, tm, D // 2), U32), pltpu.SemaphoreType.DMA(())],
        compiler_params=_cp(("arbitrary",)),
        name="moe_combine",
    )(dest.reshape(T // tm, 1, TOP_K * tm), yb, gp, x2d, m5)


def _moe(streams, router_b, wg, wu, wd):
    h_all = jnp.concatenate([s[1].reshape(-1, D_MODEL // 2) for s in streams], axis=0)
    sc_all = jnp.concatenate([s[2].reshape(-1, LANES)[:, :N_EXPERTS] for s in streams], axis=0)
    expert, gates = _route(sc_all, router_b)
    dest, block_e, n_used, n_rows = _plan(expert)
    xs = jnp.zeros((n_rows, D_MODEL // 2), U32)
    off = 0
    for s in streams:
        n = s[1].shape[0] * s[1].shape[1]
        xs = _scatter(s[1].reshape(n, D_MODEL // 2), dest[off:off + n], xs)
        off += n
    yb = _ffn(xs, block_e, n_used, wg, wu, wd)
    outs, off = [], 0
    for x, _, _, m5 in streams:
        B, L, D = x.shape
        n = B * L
        o = _combine(yb, dest[off:off + n], gates[off:off + n], x.reshape(n, D), m5, L)
        outs.append(o.reshape(B, L, D))
        off += n
    return outs


def _layer(x, xc, mod, modc, need_ctx, norm_mix, norm_ffn, w_in, gate_b, w_out, out_norm, na_q, na_k, na_rpb,
           ml_cw, ml_cb, sc_w, hy_cw, hy_cb, hy_w1, hy_b1, hy_w2, hy_b2, hy_w3, hy_freq, hy_bias,
           router_w, router_b, wg, wu, wd, rope_tabs):
    B, L, D = x.shape
    w_main = jnp.concatenate([w_in[:, :GATE_COL0], w_in[:, GATE_COL0 + ML_GATES:]], axis=1).astype(BF16)
    w_gate = jnp.zeros((D, LANES), BF16).at[:, :ML_GATES].set(w_in[:, GATE_COL0:GATE_COL0 + ML_GATES].astype(BF16))
    b_gate = jnp.zeros((1, LANES), F32).at[0, :ML_GATES].set(gate_b.astype(F32))
    gain_mix = norm_mix.reshape(1, D).astype(F32)

    p, g = _inproj(x, mod[0], mod[1], gain_mix, w_main, w_gate, b_gate)
    pc, gc = _inproj(xc, modc[0], modc[1], gain_mix, w_main, w_gate, b_gate)

    mlq, mlk, y_sc, hv, hx1, hx2 = _prep(p, ml_cw, ml_cb, sc_w, hy_cw, hy_cb, rope_tabs)
    mlqc, mlkc, yc_sc, hvc, hx1c, hx2c = _prep(pc, ml_cw, ml_cb, sc_w, hy_cw, hy_cb, None)

    y_na = _na(p, pc, na_rpb, na_q, na_k)
    hf, hb, hcf, hcb = _mlstm(mlq, mlk, p, g, mlqc, mlkc, pc, gc)
    y_hy = _hyena(hv, hx1, hx2, hy_w1, hy_b1, hy_w2, hy_b2, hy_w3, hy_freq, hy_bias)

    w_out_b = w_out.astype(BF16)
    out_gain = out_norm.reshape(1, D).astype(F32)
    g_ffn = norm_ffn.reshape(1, D).astype(F32)
    rw = jnp.zeros((D, LANES), F32).at[:, :N_EXPERTS].set(router_w.astype(F32))
    rwh = rw.astype(BF16)
    rwl = (rw - rwh.astype(F32)).astype(BF16)

    x1, h2p, sc = _merge(y_na, hf, hb, p, y_sc, y_hy, out_gain, w_out_b, x, mod[2], mod[3], mod[4], g_ffn, rwh, rwl)
    streams = [(x1, h2p, sc, mod[5])]
    if need_ctx:
        yc_na = _ctx_attn(pc, na_q, na_k)
        yc_hy = _hyena(hvc, hx1c, hx2c, hy_w1, hy_b1, hy_w2, hy_b2, hy_w3, hy_freq, hy_bias)
        xc1, h2pc, scc = _merge(yc_na, hcf, hcb, pc, yc_sc, yc_hy, out_gain, w_out_b, xc, modc[2], modc[3], modc[4],
                                g_ffn, rwh, rwl)
        streams.append((xc1, h2pc, scc, modc[5]))
    outs = _moe(streams, router_b, wg.astype(BF16), wu.astype(BF16), wd.astype(BF16))
    return outs[0], (outs[1] if need_ctx else None)


def kernel(x, c, ctx, c_ctx, w_ada, b_ada, norm_mix, norm_ffn, w_in, mlstm_gate_bias, w_out, out_norm, na_q_norm, na_k_norm, na_rpb, mlstm_conv_w, mlstm_conv_b, sconv_w, hyena_conv_w, hyena_conv_b, hyena_f_w1, hyena_f_b1, hyena_f_w2, hyena_f_b2, hyena_f_w3, hyena_f_freq, hyena_bias, router_w, router_bias, moe_w_gate, moe_w_up, moe_w_down):
    B, L, D = x.shape
    depth = w_ada.shape[0]
    xc = ctx
    s = jnp.zeros((SUBLANES, D), F32)
    s = s.at[:B].set(jax.nn.silu(c)).at[B].set(jax.nn.silu(c_ctx))
    rope_tabs = _rope_tables(L)
    for l in range(depth):
        need_ctx = l < depth - 1
        m = _ada(s, w_ada[l], b_ada[l])
        mod = [m[:B, n * D:(n + 1) * D].reshape(B, 1, D) for n in range(6)]
        modc = [jnp.broadcast_to(m[B, n * D:(n + 1) * D].reshape(1, 1, D), (B, 1, D)) for n in range(6)]
        x, xc = _layer(x, xc, mod, modc, need_ctx, norm_mix[l], norm_ffn[l], w_in[l], mlstm_gate_bias[l], w_out[l],
                       out_norm[l], na_q_norm[l], na_k_norm[l], na_rpb[l], mlstm_conv_w[l], mlstm_conv_b[l],
                       sconv_w[l], hyena_conv_w[l], hyena_conv_b[l], hyena_f_w1[l], hyena_f_b1[l], hyena_f_w2[l],
                       hyena_f_b2[l], hyena_f_w3[l], hyena_f_freq[l], hyena_bias[l], router_w, router_bias,
                       moe_w_gate[l], moe_w_up[l], moe_w_down[l], rope_tabs)
    return x
```

```python
import functools
import math

import numpy as np
import jax
import jax.numpy as jnp
from jax import lax
from jax.experimental import pallas as pl
from jax.experimental.pallas import tpu as pltpu

F32 = jnp.float32
BF16 = jnp.bfloat16
U32 = jnp.uint32

D_MODEL = 2048
GRID_W = 64
NA_HEADS, NA_HD, NA_ROWS, NA_COLS = 8, 64, 8, 16
ML_HEADS, ML_HD = 4, 128
ML_GATES = 4 * ML_HEADS
ROPE_BASE = 10000.0
MIX_W = 512
HY_ORDER, HY_BANDS, HY_FFN = 2, 8, 64
HY_EMB = 1 + 2 * HY_BANDS
HY_FAST_DECAY, HY_SLOW_DECAY, HY_TARGET = 0.3, 1.5, 0.01
N_EXPERTS, N_GROUPS, TOP_K, EXPERT_FF = 32, 8, 2, 1408
EPG = N_EXPERTS // N_GROUPS
NORM_EPS = 1e-6
GATE_COL0 = 3 * MIX_W + 4 * MIX_W
P_COLS = 13 * MIX_W
CB_NA_Q, CB_NA_K, CB_NA_V = 0, 1, 2
CB_ML_Q, CB_ML_K, CB_ML_V, CB_ML_O = 3, 4, 5, 6
CB_SC_B, CB_SC_C, CB_SC_X = 7, 8, 9
CB_HY_V, CB_HY_X1, CB_HY_X2 = 10, 11, 12

LANES = 128
SUBLANES = 8
BF16_ROWS = 16
VMEM_LIMIT = 56 * 1024 * 1024

ML_CHUNK = 256
FFT_N2 = 256
FFT_JB = SUBLANES
MOE_BLOCK = 256
NEG = -1e30


def _cp(sem, vmem=VMEM_LIMIT):
    return pltpu.CompilerParams(dimension_semantics=sem, vmem_limit_bytes=vmem)


def _dot(a, b):
    return jnp.dot(a, b, preferred_element_type=F32)


def _dot_nt(a, b):
    return lax.dot_general(a, b, (((1,), (1,)), ((), ())), preferred_element_type=F32)


def _dot_hi(a, b):
    return jnp.dot(a, b, preferred_element_type=F32, precision=lax.Precision.HIGHEST)


def _pack(lo, hi):
    a = lax.bitcast_convert_type(lo.astype(BF16).astype(F32), U32) >> 16
    b = lax.bitcast_convert_type(hi.astype(BF16).astype(F32), U32) & jnp.uint32(0xFFFF0000)
    return a | b


def _unpack(w):
    lo = lax.bitcast_convert_type(w << 16, F32)
    hi = lax.bitcast_convert_type(w & jnp.uint32(0xFFFF0000), F32)
    return lo, hi


def _block_diag_mean(width, group):
    m = np.kron(np.eye(width // group), np.full((group, group), 1.0 / group))
    return jnp.asarray(m, BF16)


def _group_rms(y, bd, gain):
    ms = _dot((y * y).astype(BF16), bd)
    return y * lax.rsqrt(ms + NORM_EPS) * gain


def _ada_kernel(s_ref, w_ref, b_ref, o_ref):
    o_ref[...] = _dot_hi(s_ref[...], w_ref[...]) + b_ref[...]


def _ada(s, w, b):
    tn = 1536
    n = w.shape[1]
    return pl.pallas_call(
        _ada_kernel,
        grid=(n // tn,),
        in_specs=[pl.BlockSpec((SUBLANES, D_MODEL), lambda j: (0, 0)),
                  pl.BlockSpec((D_MODEL, tn), lambda j: (0, j)),
                  pl.BlockSpec((1, tn), lambda j: (0, j))],
        out_specs=pl.BlockSpec((SUBLANES, tn), lambda j: (0, j)),
        out_shape=jax.ShapeDtypeStruct((SUBLANES, n), F32),
        compiler_params=_cp(("arbitrary",)),
        name="ada",
    )(s, w, b.reshape(1, n))


def _inproj_kernel(x_ref, sh_ref, sc_ref, gain_ref, w_ref, wg_ref, bg_ref, p_ref, g_ref, h_scr):
    @pl.when(pl.program_id(2) == 0)
    def _():
        x = x_ref[0]
        ms = jnp.mean(x * x, axis=-1, keepdims=True)
        y = x * lax.rsqrt(ms + NORM_EPS) * gain_ref[...]
        h = (y * (1.0 + sc_ref[0]) + sh_ref[0]).astype(BF16)
        h_scr[...] = h
        g_ref[0] = _dot(h, wg_ref[...]) + bg_ref[...]

    p_ref[0] = _dot(h_scr[...], w_ref[...]).astype(BF16)


def _inproj(x, shift, scale, gain, w_main, w_gate, b_gate):
    B, L, D = x.shape
    tm = min(512, L)
    tn = P_COLS // 4
    return pl.pallas_call(
        _inproj_kernel,
        grid=(B, L // tm, P_COLS // tn),
        in_specs=[pl.BlockSpec((1, tm, D), lambda b, i, j: (b, i, 0)),
                  pl.BlockSpec((1, 1, D), lambda b, i, j: (b, 0, 0)),
                  pl.BlockSpec((1, 1, D), lambda b, i, j: (b, 0, 0)),
                  pl.BlockSpec((1, D), lambda b, i, j: (0, 0)),
                  pl.BlockSpec((D, tn), lambda b, i, j: (0, j)),
                  pl.BlockSpec((D, LANES), lambda b, i, j: (0, 0)),
                  pl.BlockSpec((1, LANES), lambda b, i, j: (0, 0))],
        out_specs=[pl.BlockSpec((1, tm, tn), lambda b, i, j: (b, i, j)),
                   pl.BlockSpec((1, tm, LANES), lambda b, i, j: (b, i, 0))],
        out_shape=[jax.ShapeDtypeStruct((B, L, P_COLS), BF16),
                   jax.ShapeDtypeStruct((B, L, LANES), F32)],
        scratch_shapes=[pltpu.VMEM((tm, D), BF16)],
        compiler_params=_cp(("arbitrary", "arbitrary", "arbitrary")),
        name="inproj",
    )(x, shift, scale, gain, w_main, w_gate, b_gate)


_CONV_BLOCKS = (CB_ML_Q, CB_ML_K, CB_SC_C, CB_SC_X, CB_HY_V, CB_HY_X1, CB_HY_X2)


def _prep_kernel(*refs, rope, tm):
    n_conv = len(_CONV_BLOCKS)
    main = refs[0:n_conv]
    prev = refs[n_conv:2 * n_conv]
    nxt = refs[2 * n_conv:3 * n_conv]
    k = 3 * n_conv
    scb_ref, mlw_ref, mlb_ref, scw_ref, hyw_ref, hyb_ref = refs[k:k + 6]
    k += 6
    if rope:
        cos_ref, sin_ref = refs[k:k + 2]
        k += 2
    q_ref, k_ref, sc_ref, hv_ref, hx1_ref, hx2_ref = refs[k:k + 6]

    i = pl.program_id(1)
    first = i == 0
    last = i == pl.num_programs(1) - 1
    rid = lax.broadcasted_iota(jnp.int32, (tm, MIX_W), 0)

    def rows(idx):
        xm = main[idx][0].astype(F32)
        xp = prev[idx][0].astype(F32)[BF16_ROWS - 1:BF16_ROWS, :]
        xn = nxt[idx][0].astype(F32)[0:1, :]
        return xm, xp, xn

    def conv(xm, xp, xn, w, b):
        xp = jnp.where(first, 0.0, xp)
        xn = jnp.where(last, 0.0, xn)
        up = jnp.where(rid == 0, xp, pltpu.roll(xm, 1, axis=0))
        dn = jnp.where(rid == tm - 1, xn, pltpu.roll(xm, tm - 1, axis=0))
        y = up * w[0:1] + xm * w[1:2] + dn * w[2:3]
        return y if b is None else y + b

    def silu(v):
        return v * jax.nn.sigmoid(v)

    def rot(t):
        if not rope:
            return t
        lane = lax.broadcasted_iota(jnp.int32, (tm, LANES), 1)
        first_half = (lane % 64) < 32
        cos, sin = cos_ref[...], sin_ref[...]
        outs = []
        for h in range(ML_HEADS):
            th = t[:, h * ML_HD:(h + 1) * ML_HD]
            partner = jnp.where(first_half, pltpu.roll(th, LANES - 32, axis=1), pltpu.roll(th, 32, axis=1))
            outs.append(th * cos + partner * sin)
        return jnp.concatenate(outs, axis=1)

    mlw, mlb = mlw_ref[...], mlb_ref[...]
    q = silu(conv(*rows(0), mlw[:, :MIX_W], mlb[:, :MIX_W]))
    kk = silu(conv(*rows(1), mlw[:, MIX_W:], mlb[:, MIX_W:]))
    q_ref[0] = rot(q).astype(BF16)
    k_ref[0] = (rot(kk) * ML_HD ** -0.5).astype(BF16)

    cm, cp_, cn = rows(2)
    xm, xp, xn = rows(3)
    sc = conv(cm * xm, cp_ * xp, cn * xn, scw_ref[...], None)
    sc_ref[0] = (scb_ref[0].astype(F32) * sc).astype(BF16)

    hyw, hyb = hyw_ref[...], hyb_ref[...]
    for n, o_ref in enumerate((hv_ref, hx1_ref, hx2_ref)):
        o_ref[0] = conv(*rows(4 + n), hyw[:, n * MIX_W:(n + 1) * MIX_W], hyb[:, n * MIX_W:(n + 1) * MIX_W])


def _prep(p, ml_w, ml_b, sc_w, hy_w, hy_b, rope_tabs):
    B, L, _ = p.shape
    tm = 256
    nt = L // tm
    hb = tm // BF16_ROWS
    n_halo = L // BF16_ROWS
    rope = rope_tabs is not None

    in_specs, args = [], []
    for cb in _CONV_BLOCKS:
        in_specs.append(pl.BlockSpec((1, tm, MIX_W), lambda b, i, cb=cb: (b, i, cb)))
        args.append(p)
    for cb in _CONV_BLOCKS:
        in_specs.append(pl.BlockSpec((1, BF16_ROWS, MIX_W), lambda b, i, cb=cb: (b, jnp.maximum(i * hb - 1, 0), cb)))
        args.append(p)
    for cb in _CONV_BLOCKS:
        in_specs.append(pl.BlockSpec((1, BF16_ROWS, MIX_W),
                                     lambda b, i, cb=cb: (b, jnp.minimum((i + 1) * hb, n_halo - 1), cb)))
        args.append(p)
    in_specs.append(pl.BlockSpec((1, tm, MIX_W), lambda b, i: (b, i, CB_SC_B)))
    args.append(p)
    for w in (ml_w, ml_b.reshape(1, -1), sc_w, hy_w, hy_b.reshape(1, -1)):
        in_specs.append(pl.BlockSpec(w.shape, lambda b, i: (0, 0)))
        args.append(w)
    if rope:
        for t in rope_tabs:
            in_specs.append(pl.BlockSpec((tm, LANES), lambda b, i: (i, 0)))
            args.append(t)
    blk = pl.BlockSpec((1, tm, MIX_W), lambda b, i: (b, i, 0))
    return pl.pallas_call(
        functools.partial(_prep_kernel, rope=rope, tm=tm),
        grid=(B, nt),
        in_specs=in_specs,
        out_specs=[blk] * 6,
        out_shape=[jax.ShapeDtypeStruct((B, L, MIX_W), BF16)] * 3 + [jax.ShapeDtypeStruct((B, L, MIX_W), F32)] * 3,
        compiler_params=_cp(("arbitrary", "arbitrary")),
        name="prep",
    )(*args)


def _rope_tables(L):
    quarter = ML_HD // 4
    pos = jnp.arange(L, dtype=jnp.int32)
    inv = ROPE_BASE ** (-jnp.arange(quarter, dtype=F32) / quarter)
    ar = (pos // GRID_W).astype(F32)[:, None] * inv[None, :]
    ac = (pos % GRID_W).astype(F32)[:, None] * inv[None, :]
    cos = jnp.concatenate([jnp.cos(ar), jnp.cos(ar), jnp.cos(ac), jnp.cos(ac)], axis=1)
    sin = jnp.concatenate([-jnp.sin(ar), jnp.sin(ar), -jnp.sin(ac), jnp.sin(ac)], axis=1)
    return cos, sin


NA_GROUP = 8
NA_Q = NA_GROUP * GRID_W
NA_KEYS = 2 * NA_Q


def _head_norm(x_bf16, bd, gain):
    x = x_bf16.astype(F32)
    return _group_rms(x, bd, gain)


def _pair_attention(qn, keys, vals, biases):
    lane = lax.broadcasted_iota(jnp.int32, qn.shape, 1)
    outs = []
    for half in range(2):
        sel = (lane < NA_HD) if half == 0 else (lane >= NA_HD)
        qh = jnp.where(sel, qn, jnp.zeros_like(qn))
        scores = []
        for kk, bias in zip(keys, biases[half]):
            s = _dot_nt(qh, kk)
            scores.append(s if bias is None else s + bias)
        m = functools.reduce(jnp.maximum, [jnp.max(s, axis=-1, keepdims=True) for s in scores])
        es = [jnp.exp(s - m) for s in scores]
        den = functools.reduce(jnp.add, [jnp.sum(e, axis=-1, keepdims=True) for e in es])
        o = functools.reduce(jnp.add, [_dot(e.astype(BF16), v) for e, v in zip(es, vals)])
        outs.append(o / den)
    return jnp.where(lane < NA_HD, outs[0], outs[1])


def _na_kernel(q_ref, kp_ref, kc_ref, kn_ref, vp_ref, vc_ref, vn_ref, ck_ref, cv_ref, tab_ref,
               qg_ref, kg_ref, bd_ref, o_ref):
    bd = bd_ref[...]
    half_q = NA_Q // 2
    qn = (_head_norm(q_ref[0], bd, qg_ref[...]) * NA_HD ** -0.5).astype(BF16)
    k_raw = jnp.concatenate([kp_ref[0][half_q:], kc_ref[0], kn_ref[0][:half_q]], axis=0)
    kun = _head_norm(k_raw, bd, kg_ref[...]).astype(BF16)
    vun = jnp.concatenate([vp_ref[0][half_q:], vc_ref[0], vn_ref[0][:half_q]], axis=0)
    ckn = _head_norm(ck_ref[0], bd, kg_ref[...]).astype(BF16)
    o = _pair_attention(qn, [kun, ckn], [vun, cv_ref[0]], [[tab_ref[0, 0], None], [tab_ref[0, 1], None]])
    o_ref[0] = o.astype(BF16)


def _na_table(rpb, rows):
    wr = NA_ROWS
    qc = np.arange(GRID_W)[:, None]
    kc = np.arange(GRID_W)[None, :]
    col0 = np.clip(qc - NA_COLS // 2, 0, GRID_W - NA_COLS)
    col_ok = (kc >= col0) & (kc < col0 + NA_COLS)
    nd, ndc = 2 * wr - 1, 2 * NA_COLS - 1
    dc = kc - qc + NA_COLS - 1
    onehot = (np.arange(ndc)[:, None, None] == dc[None]) & col_ok[None]
    cb = jnp.dot(rpb.reshape(NA_HEADS * nd, ndc), jnp.asarray(onehot.reshape(ndc, -1), F32),
                 precision=lax.Precision.HIGHEST).reshape(NA_HEADS, nd, GRID_W, GRID_W)
    cb = jnp.where(col_ok[None, None], cb, NEG)
    cb = jnp.concatenate([cb, jnp.full((NA_HEADS, 1, GRID_W, GRID_W), NEG, F32)], axis=1)
    d_idx = np.full((3, NA_GROUP, 2 * NA_GROUP), nd, np.int32)
    for kind, R in enumerate((0, NA_GROUP, rows - NA_GROUP)):
        for rr in range(NA_GROUP):
            r = R + rr
            r0 = int(np.clip(r - wr // 2, 0, rows - wr))
            for u in range(2 * NA_GROUP):
                key_row = R - NA_GROUP // 2 + u
                if r0 <= key_row <= r0 + wr - 1:
                    d_idx[kind, rr, u] = key_row - r + wr - 1
    t = jnp.take(cb, jnp.asarray(d_idx.reshape(-1)), axis=1)
    t = t.reshape(NA_HEADS, 3, NA_GROUP, 2 * NA_GROUP, GRID_W, GRID_W)
    return jnp.transpose(t, (1, 0, 2, 4, 3, 5)).reshape(3, NA_HEADS, NA_Q, NA_KEYS)


def _na(p, pc, rpb, q_gain, k_gain):
    B, L, _ = p.shape
    ctx = pc.shape[1]
    ng = L // NA_Q
    rows = L // GRID_W
    tab = _na_table(rpb.astype(F32), rows)
    qg = jnp.tile(q_gain.astype(F32), 2).reshape(1, LANES)
    kg = jnp.tile(k_gain.astype(F32), 2).reshape(1, LANES)
    bd = _block_diag_mean(LANES, NA_HD)
    npair = NA_HEADS // 2
    cpb = MIX_W // LANES

    def kind(i):
        return jnp.where(i == 0, 0, jnp.where(i == ng - 1, 2, 1))

    def blk(cb, which):
        def idx(hp, b, i):
            if which == 0:
                r = i
            elif which < 0:
                r = jnp.maximum(i - 1, 0)
            else:
                r = jnp.minimum(i + 1, ng - 1)
            return (b, r, cb * cpb + hp)
        return pl.BlockSpec((1, NA_Q, LANES), idx)

    def cblk(cb):
        return pl.BlockSpec((1, ctx, LANES), lambda hp, b, i: (b, 0, cb * cpb + hp))

    small = lambda shape: pl.BlockSpec(shape, lambda hp, b, i: (0, 0))
    return pl.pallas_call(
        _na_kernel,
        grid=(npair, B, ng),
        in_specs=[blk(CB_NA_Q, 0), blk(CB_NA_K, -1), blk(CB_NA_K, 0), blk(CB_NA_K, 1),
                  blk(CB_NA_V, -1), blk(CB_NA_V, 0), blk(CB_NA_V, 1), cblk(CB_NA_K), cblk(CB_NA_V),
                  pl.BlockSpec((1, 2, NA_Q, NA_KEYS), lambda hp, b, i: (kind(i), hp, 0, 0)),
                  small((1, LANES)), small((1, LANES)), small((LANES, LANES))],
        out_specs=pl.BlockSpec((1, NA_Q, LANES), lambda hp, b, i: (b, i, hp)),
        out_shape=jax.ShapeDtypeStruct((B, L, MIX_W), BF16),
        compiler_params=_cp(("arbitrary", "arbitrary", "arbitrary")),
        name="na",
    )(p, p, p, p, p, p, p, pc, pc, tab, qg, kg, bd)


def _ctx_attn_kernel(q_ref, k_ref, v_ref, qg_ref, kg_ref, bd_ref, o_ref):
    bd = bd_ref[...]
    qn = (_head_norm(q_ref[0], bd, qg_ref[...]) * NA_HD ** -0.5).astype(BF16)
    kn = _head_norm(k_ref[0], bd, kg_ref[...]).astype(BF16)
    o = _pair_attention(qn, [kn], [v_ref[0]], [[None], [None]])
    o_ref[0] = o.astype(BF16)


def _ctx_attn(pc, q_gain, k_gain):
    B, ctx, _ = pc.shape
    qg = jnp.tile(q_gain.astype(F32), 2).reshape(1, LANES)
    kg = jnp.tile(k_gain.astype(F32), 2).reshape(1, LANES)
    bd = _block_diag_mean(LANES, NA_HD)
    cpb = MIX_W // LANES
    blk = lambda cb: pl.BlockSpec((1, ctx, LANES), lambda hp, b: (b, 0, cb * cpb + hp))
    small = lambda shape: pl.BlockSpec(shape, lambda hp, b: (0, 0))
    return pl.pallas_call(
        _ctx_attn_kernel,
        grid=(NA_HEADS // 2, B),
        in_specs=[blk(CB_NA_Q), blk(CB_NA_K), blk(CB_NA_V), small((1, LANES)), small((1, LANES)),
                  small((LANES, LANES))],
        out_specs=pl.BlockSpec((1, ctx, LANES), lambda hp, b: (b, 0, hp)),
        out_shape=jax.ShapeDtypeStruct((B, ctx, MIX_W), BF16),
        compiler_params=_cp(("arbitrary", "arbitrary")),
        name="ctx_attn",
    )(pc, pc, pc, qg, kg, bd)


def _log_sigmoid(x):
    return jnp.minimum(x, 0.0) - jnp.log(1.0 + jnp.exp(-jnp.abs(x)))


def _split_dot(a, b_f32, a_is_const):
    if a_is_const:
        hi = b_f32.astype(BF16)
        lo = (b_f32 - hi.astype(F32)).astype(BF16)
        return _dot(a, hi) + _dot(a, lo)
    hi = a.astype(BF16)
    lo = (a - hi.astype(F32)).astype(BF16)
    return _dot(hi, b_f32) + _dot(lo, b_f32)


def _mlstm_kernel(*refs):
    (qf, kf, vf, gf, gtf, qb, kb, vb, gb, gtb, qc, kc, vc, gc, gtc, tri_ref, trit_ref,
     hf_ref, hb_ref, hcf_ref, hcb_ref, c_scr, m_scr) = refs
    cl = ML_CHUNK
    s = pl.program_id(1)

    @pl.when(s == 0)
    def _():
        c_scr[...] = jnp.zeros_like(c_scr)
        m_scr[...] = jnp.zeros_like(m_scr)

    tri, trit = tri_ref[...], trit_ref[...]
    row_i = lax.broadcasted_iota(jnp.int32, (cl, cl), 0)
    col_i = lax.broadcasted_iota(jnp.int32, (cl, cl), 1)
    lane = lax.broadcasted_iota(jnp.int32, (cl, LANES), 1)
    ones_col = jnp.where(lane == 0, 1.0, 0.0).astype(BF16)

    def stream(d, q_ref, k_ref, v_ref, g_ref, gt_ref, out_ref):
        g = g_ref[0]
        gt = gt_ref[0]
        lf_cols = _log_sigmoid(g)
        lf_rows = _log_sigmoid(gt)
        if d == 0:
            b_cols = _split_dot(tri, lf_cols, True)
            b_rows = _split_dot(lf_rows, trit, False)
            mask = col_i <= row_i
        else:
            b_cols = _split_dot(trit, lf_cols, True)
            b_rows = _split_dot(lf_rows, tri, False)
            mask = col_i >= row_i
        outs = []
        for h in range(ML_HEADS):
            ci = (2 * d) * ML_HEADS + h
            cf = (2 * d + 1) * ML_HEADS + h
            q = q_ref[0][:, h * ML_HD:(h + 1) * ML_HD]
            k = k_ref[0][:, h * ML_HD:(h + 1) * ML_HD]
            v_aug = jnp.concatenate([v_ref[0][:, h * ML_HD:(h + 1) * ML_HD], ones_col], axis=1)
            b_col = b_cols[:, cf:cf + 1]
            b_row = b_rows[cf:cf + 1, :]
            li_row = gt[ci:ci + 1, :]
            m_prev = m_scr[d, h][0:1, 0:1]
            ct = c_scr[d, h]
            dm = jnp.where(mask, b_col - b_row + li_row, -jnp.inf)
            inter = b_col + m_prev
            m_t = jnp.maximum(jnp.max(dm, axis=-1, keepdims=True), inter)
            sc = _dot_nt(q, k) * jnp.exp(dm - m_t)
            gi = jnp.exp(inter - m_t)
            nd = _dot(sc.astype(BF16), v_aug) + gi * _dot(q, ct.astype(BF16))
            num = nd[:, :ML_HD]
            den = nd[:, ML_HD:ML_HD + 1]
            outs.append(num / jnp.maximum(jnp.abs(den), jnp.exp(-m_t)))
            b_last = b_row[:, cl - 1:cl] if d == 0 else b_row[:, 0:1]
            tail = b_last - b_row + li_row
            m_new = jnp.maximum(b_last + m_prev, jnp.max(tail, axis=-1, keepdims=True))
            w_row = jnp.exp(tail - m_new)
            decay = jnp.exp(b_last + m_prev - m_new)
            kt = k.astype(F32).T
            c_scr[d, h] = decay * ct + _dot((kt * w_row).astype(BF16), v_aug)
            m_scr[d, h] = jnp.broadcast_to(m_new, (SUBLANES, LANES))
        out_ref[0] = jnp.concatenate(outs, axis=1).astype(BF16)

    @pl.when(s == 0)
    def _():
        stream(0, qc, kc, vc, gc, gtc, hcf_ref)
        stream(1, qc, kc, vc, gc, gtc, hcb_ref)

    @pl.when(s > 0)
    def _():
        stream(0, qf, kf, vf, gf, gtf, hf_ref)
        stream(1, qb, kb, vb, gb, gtb, hb_ref)


def _mlstm(mlq, mlk, p, g, mlqc, mlkc, pc, gc):
    B, L, _ = mlq.shape
    cl = ML_CHUNK
    assert pc.shape[1] == cl
    n = L // cl
    gt = jnp.swapaxes(g[..., :ML_GATES], 1, 2)
    gtc = jnp.swapaxes(gc[..., :ML_GATES], 1, 2)
    tri = jnp.asarray(np.tril(np.ones((cl, cl))), BF16)
    trit = jnp.asarray(np.triu(np.ones((cl, cl))), BF16)

    fwd = lambda s: jnp.maximum(s - 1, 0)
    bwd = lambda s: jnp.clip(n - s, 0, n - 1)

    def specs(sel, vcol):
        return [pl.BlockSpec((1, cl, MIX_W), lambda b, s: (b, sel(s), 0)),
                pl.BlockSpec((1, cl, MIX_W), lambda b, s: (b, sel(s), 0)),
                pl.BlockSpec((1, cl, MIX_W), lambda b, s: (b, sel(s), vcol)),
                pl.BlockSpec((1, cl, LANES), lambda b, s: (b, sel(s), 0)),
                pl.BlockSpec((1, ML_GATES, cl), lambda b, s: (b, 0, sel(s)))]

    zero = lambda s: 0
    const = pl.BlockSpec((cl, cl), lambda b, s: (0, 0))
    return pl.pallas_call(
        _mlstm_kernel,
        grid=(B, n + 1),
        in_specs=specs(fwd, CB_ML_V) + specs(bwd, CB_ML_V) + specs(zero, CB_ML_V) + [const, const],
        out_specs=[pl.BlockSpec((1, cl, MIX_W), lambda b, s: (b, fwd(s), 0)),
                   pl.BlockSpec((1, cl, MIX_W), lambda b, s: (b, bwd(s), 0)),
                   pl.BlockSpec((1, cl, MIX_W), lambda b, s: (b, 0, 0)),
                   pl.BlockSpec((1, cl, MIX_W), lambda b, s: (b, 0, 0))],
        out_shape=[jax.ShapeDtypeStruct((B, L, MIX_W), BF16)] * 2 + [jax.ShapeDtypeStruct((B, cl, MIX_W), BF16)] * 2,
        scratch_shapes=[pltpu.VMEM((2, ML_HEADS, ML_HD, 2 * ML_HD), F32),
                        pltpu.VMEM((2, ML_HEADS, SUBLANES, LANES), F32)],
        compiler_params=_cp(("arbitrary", "arbitrary")),
        name="mlstm",
    )(mlq, mlk, p, g, gt, mlq, mlk, p, g, gt, mlqc, mlkc, pc, gc, gtc, tri, trit)


def _filter_kernel(z_ref, w1_ref, b1_ref, w2_ref, b2_ref, w3_ref, fr_ref, dl_ref, j_ref, hf_ref, hb_ref, *, length):
    z = z_ref[...]
    fr = fr_ref[...]
    h1 = jnp.sin(fr[0:1] * (_dot_hi(z, w1_ref[...]) + b1_ref[...]))
    h2 = jnp.sin(fr[1:2] * (_dot_hi(h1, w2_ref[...]) + b2_ref[...]))
    h = _dot_hi(h2, w3_ref[...])
    t = z[:, 0:1]
    decay = jnp.exp(-t * dl_ref[...])
    fwd, bwd = [], []
    for o in range(HY_ORDER):
        fwd.append(h[:, (2 * o) * MIX_W:(2 * o + 1) * MIX_W] * decay)
        bwd.append(h[:, (2 * o + 1) * MIX_W:(2 * o + 2) * MIX_W] * decay)
    hf_ref[...] = jnp.concatenate(fwd, axis=1)
    hb = jnp.concatenate(bwd, axis=1)
    jm = j_ref[...]
    a = hb.astype(BF16)
    r1 = hb - a.astype(F32)
    b = r1.astype(BF16)
    c = (r1 - b.astype(F32)).astype(BF16)
    hb_ref[...] = _dot(jm, a) + _dot(jm, b) + _dot(jm, c)


def _hyena_filter(L, w1, b1, w2, b2, w3, freq):
    tb = FFT_N2
    nb = L // tb
    t = (jnp.arange(L, dtype=F32) / L)[:, None]
    bands = jnp.linspace(1e-4, HY_BANDS - 1, HY_BANDS, dtype=F32)
    ang = 2 * math.pi * t * bands
    z = jnp.concatenate([t, jnp.cos(ang), jnp.sin(ang), jnp.zeros((L, LANES - HY_EMB), F32)], axis=-1)
    deltas = np.abs(np.linspace(math.log(HY_TARGET) / HY_SLOW_DECAY, math.log(HY_TARGET) / HY_FAST_DECAY, MIX_W,
                                dtype=np.float32)).reshape(1, MIX_W)
    w1p = jnp.zeros((LANES, HY_FFN), F32).at[:HY_EMB].set(w1.astype(F32))
    jm = jnp.asarray(np.eye(tb)[::-1].copy(), BF16)
    width = HY_ORDER * MIX_W
    full = lambda a: pl.BlockSpec(a.shape, lambda i: (0,) * a.ndim)
    args = (z, w1p, b1.reshape(1, -1).astype(F32), w2.astype(F32), b2.reshape(1, -1).astype(F32),
            w3.astype(F32), freq.astype(F32), jnp.asarray(deltas), jm)
    hf, hb_rev = pl.pallas_call(
        functools.partial(_filter_kernel, length=L),
        grid=(nb,),
        in_specs=[pl.BlockSpec((tb, LANES), lambda i: (i, 0))] + [full(a) for a in args[1:]],
        out_specs=[pl.BlockSpec((tb, width), lambda i: (i, 0)),
                   pl.BlockSpec((tb, width), lambda i: (nb - 1 - i, 0))],
        out_shape=[jax.ShapeDtypeStruct((L, width), F32)] * 2,
        compiler_params=_cp(("arbitrary",)),
        name="hy_filter",
    )(*args)
    return hf, hb_rev


def _fft_consts(n1, n1v):
    jb = FFT_JB
    th1 = 2 * np.pi * np.outer(np.arange(n1), np.arange(n1v)) / n1
    f1 = np.kron(np.concatenate([np.cos(th1), -np.sin(th1)], axis=0), np.eye(jb))
    i1 = np.kron(np.concatenate([np.cos(th1.T), -np.sin(th1.T)], axis=1), np.eye(jb))
    th2 = 2 * np.pi * np.outer(np.arange(FFT_N2), np.arange(FFT_N2)) / FFT_N2
    c2, s2 = np.cos(th2), np.sin(th2)
    f2 = np.block([[c2, s2], [-s2, c2]])
    i2 = np.block([[c2, -s2], [s2, c2]])
    prod = (jnp.arange(n1, dtype=jnp.int32)[:, None] * jnp.arange(FFT_N2, dtype=jnp.int32)[None, :]).astype(F32)
    tht = prod * (2 * math.pi / (n1 * FFT_N2))
    twr = jnp.broadcast_to(jnp.cos(tht)[:, :, None], (n1, FFT_N2, LANES))
    twi = jnp.broadcast_to(-jnp.sin(tht)[:, :, None], (n1, FFT_N2, LANES))
    return (jnp.asarray(f1, BF16), jnp.asarray(i1, BF16), jnp.asarray(f2, BF16), jnp.asarray(i2, BF16), twr, twi)


def _fft_outer_kernel(u_ref, f1_ref, a_ref):
    n1v, jb, ct = u_ref.shape[1:]
    n1 = a_ref.shape[1]
    u = u_ref[0].reshape(n1v * jb, ct).astype(BF16)
    res = _dot(f1_ref[...], u)
    half = n1 * jb
    a_ref[0] = _pack(res[:half], res[half:]).reshape(n1, jb, ct)


def _fft_outer(u4, f1, n1):
    B, n1v, _, C = u4.shape
    ct = MIX_W
    return pl.pallas_call(
        _fft_outer_kernel,
        grid=(B, FFT_N2 // FFT_JB, C // ct),
        in_specs=[pl.BlockSpec((1, n1v, FFT_JB, ct), lambda b, j, c: (b, 0, j, c)),
                  pl.BlockSpec(f1.shape, lambda b, j, c: (0, 0))],
        out_specs=pl.BlockSpec((1, n1, FFT_JB, ct), lambda b, j, c: (b, 0, j, c)),
        out_shape=jax.ShapeDtypeStruct((B, n1, FFT_N2, C), U32),
        compiler_params=_cp(("arbitrary", "arbitrary", "arbitrary")),
        name="fft_outer",
    )(u4, f1)


def _twiddle(re, im, twr, twi, conj):
    reps = re.shape[1] // LANES
    tr = jnp.concatenate([twr] * reps, axis=1)
    ti = jnp.concatenate([twi] * reps, axis=1)
    if conj:
        ti = -ti
    return re * tr - im * ti, re * ti + im * tr


def _filter_spec_kernel(a_ref, twr_ref, twi_ref, f2_ref, kf_ref):
    re, im = _unpack(a_ref[0, 0])
    re, im = _twiddle(re, im, twr_ref[0], twi_ref[0], False)
    x = _dot(f2_ref[...], jnp.concatenate([re, im], axis=0).astype(BF16))
    kf_ref[0] = _pack(x[:FFT_N2], x[FFT_N2:])


def _filter_spec(a, twr, twi, f2):
    _, n1, _, C = a.shape
    return pl.pallas_call(
        _filter_spec_kernel,
        grid=(n1,),
        in_specs=[pl.BlockSpec((1, 1, FFT_N2, C), lambda k: (0, k, 0, 0)),
                  pl.BlockSpec((1, FFT_N2, LANES), lambda k: (k, 0, 0)),
                  pl.BlockSpec((1, FFT_N2, LANES), lambda k: (k, 0, 0)),
                  pl.BlockSpec(f2.shape, lambda k: (0, 0))],
        out_specs=pl.BlockSpec((1, FFT_N2, C), lambda k: (k, 0, 0)),
        out_shape=jax.ShapeDtypeStruct((n1, FFT_N2, C), U32),
        compiler_params=_cp(("arbitrary",)),
        name="filter_spec",
    )(a, twr, twi, f2)


def _fft_mid_kernel(a_ref, kf_ref, twr_ref, twi_ref, f2_ref, i2_ref, d_ref):
    nb = a_ref.shape[0]
    twr, twi = twr_ref[0], twi_ref[0]
    res, ims = [], []
    for b in range(nb):
        re, im = _unpack(a_ref[b, 0])
        re, im = _twiddle(re, im, twr, twi, False)
        res.append(re)
        ims.append(im)
    stacked = jnp.concatenate([jnp.concatenate(res, axis=1), jnp.concatenate(ims, axis=1)], axis=0)
    x = _dot(f2_ref[...], stacked.astype(BF16))
    xr, xi = x[:FFT_N2], x[FFT_N2:]
    kr, ki = _unpack(kf_ref[0])
    kr = jnp.concatenate([kr] * nb, axis=1)
    ki = jnp.concatenate([ki] * nb, axis=1)
    zr = xr * kr - xi * ki
    zi = xr * ki + xi * kr
    y = _dot(i2_ref[...], jnp.concatenate([zr, zi], axis=0).astype(BF16))
    yr, yi = _twiddle(y[:FFT_N2], y[FFT_N2:], twr, twi, True)
    c = a_ref.shape[-1]
    for b in range(nb):
        d_ref[b, 0] = _pack(yr[:, b * c:(b + 1) * c], yi[:, b * c:(b + 1) * c])


def _fft_mid(a, kf, order, twr, twi, f2, i2):
    B, n1, _, C = a.shape
    return pl.pallas_call(
        _fft_mid_kernel,
        grid=(n1,),
        in_specs=[pl.BlockSpec((B, 1, FFT_N2, C), lambda k: (0, k, 0, 0)),
                  pl.BlockSpec((1, FFT_N2, C), lambda k: (k, 0, order)),
                  pl.BlockSpec((1, FFT_N2, LANES), lambda k: (k, 0, 0)),
                  pl.BlockSpec((1, FFT_N2, LANES), lambda k: (k, 0, 0)),
                  pl.BlockSpec(f2.shape, lambda k: (0, 0)),
                  pl.BlockSpec(i2.shape, lambda k: (0, 0))],
        out_specs=pl.BlockSpec((B, 1, FFT_N2, C), lambda k: (0, k, 0, 0)),
        out_shape=jax.ShapeDtypeStruct((B, n1, FFT_N2, C), U32),
        compiler_params=_cp(("arbitrary",)),
        name="fft_mid",
    )(a, kf, twr, twi, f2, i2)


def _fft_inv_outer_kernel(d_ref, i1_ref, u_ref, x_ref, bias_ref, y_ref, *, scale):
    n1, jb, ct = d_ref.shape[1:]
    n1v = y_ref.shape[1]
    re, im = _unpack(d_ref[0].reshape(n1 * jb, ct))
    conv = _dot(i1_ref[...], jnp.concatenate([re, im], axis=0).astype(BF16)) * scale
    u = u_ref[0].reshape(n1v * jb, ct)
    y = x_ref[0].reshape(n1v * jb, ct) * (conv + bias_ref[...] * u)
    y_ref[0] = y.reshape(n1v, jb, ct)


def _fft_inv_outer(d, i1, u4, x4, bias, n_fft):
    B, n1, _, C = d.shape
    n1v = u4.shape[1]
    ct = MIX_W
    sig = pl.BlockSpec((1, n1v, FFT_JB, ct), lambda b, j, c: (b, 0, j, c))
    return pl.pallas_call(
        functools.partial(_fft_inv_outer_kernel, scale=1.0 / n_fft),
        grid=(B, FFT_N2 // FFT_JB, C // ct),
        in_specs=[pl.BlockSpec((1, n1, FFT_JB, ct), lambda b, j, c: (b, 0, j, c)),
                  pl.BlockSpec(i1.shape, lambda b, j, c: (0, 0)),
                  sig, sig,
                  pl.BlockSpec((1, ct), lambda b, j, c: (0, c))],
        out_specs=sig,
        out_shape=jax.ShapeDtypeStruct(u4.shape, F32),
        compiler_params=_cp(("arbitrary", "arbitrary", "arbitrary")),
        name="fft_inv_outer",
    )(d, i1, u4, x4, bias)


def _hyena(hv, hx1, hx2, w1, b1, w2, b2, w3, freq, bias):
    B, L, C = hv.shape
    lp = max(L, 8 * FFT_N2)
    n_fft = 2 * lp
    n1 = n_fft // FFT_N2
    n1v = lp // FFT_N2
    f1s, i1s, f2, i2, twr, twi = _fft_consts(n1, n1v)
    f1full = _fft_consts(n1, n1)[0]

    hf, hb_rev = _hyena_filter(L, w1, b1, w2, b2, w3, freq)
    width = HY_ORDER * C
    kern = jnp.concatenate([hf, jnp.zeros((n_fft - 2 * L + 1, width), F32), hb_rev[:L - 1]], axis=0)
    ka = _fft_outer(kern.reshape(1, n1, FFT_N2, width), f1full, n1)
    kf = _filter_spec(ka, twr, twi, f2)

    def pad(t):
        if lp != L:
            t = jnp.concatenate([t, jnp.zeros((B, lp - L, C), t.dtype)], axis=1)
        return t.reshape(B, n1v, FFT_N2, C)

    y = pad(hv)
    for o, xg in enumerate((hx1, hx2)):
        a = _fft_outer(y, f1s, n1)
        d = _fft_mid(a, kf, o, twr, twi, f2, i2)
        y = _fft_inv_outer(d, i1s, y, pad(xg), bias[o].reshape(1, C).astype(F32), n_fft)
    return y.reshape(B, lp, C)[:, :L]


ROUTE_STRIDE = 32


def _route_lanes(v):
    v = v.astype(F32)
    parts = []
    for j in range(EPG):
        col = v[..., j::EPG]
        pad = jnp.zeros(col.shape[:-1] + (ROUTE_STRIDE - N_GROUPS,), F32)
        parts.append(jnp.concatenate([col, pad], axis=-1))
    return jnp.concatenate(parts, axis=-1)


def _route_tile(scores, rb):
    tm = scores.shape[0]
    lane = lax.broadcasted_iota(jnp.int32, (tm, LANES), 1)
    sel = scores + rb
    a = [sel] + [pltpu.roll(sel, LANES - ROUTE_STRIDE * j, axis=1) for j in range(1, EPG)]
    u = [scores] + [pltpu.roll(scores, LANES - ROUTE_STRIDE * j, axis=1) for j in range(1, EPG)]
    hi01, lo01 = jnp.maximum(a[0], a[1]), jnp.minimum(a[0], a[1])
    hi23, lo23 = jnp.maximum(a[2], a[3]), jnp.minimum(a[2], a[3])
    gs = jnp.maximum(hi01, hi23) + jnp.maximum(jnp.minimum(hi01, hi23), jnp.maximum(lo01, lo23))
    gs = jnp.where(lane < N_GROUPS, gs, -jnp.inf)
    mx = jnp.max(gs, axis=-1, keepdims=True)
    gidx = jnp.min(jnp.where(gs == mx, lane, LANES), axis=-1, keepdims=True)
    pick = lane == gidx
    v = [jnp.sum(jnp.where(pick, t, 0.0), axis=-1, keepdims=True) for t in a]
    w = [jnp.sum(jnp.where(pick, t, 0.0), axis=-1, keepdims=True) for t in u]
    b1, i1, w1 = v[0], jnp.zeros_like(gidx), w[0]
    for j in range(1, EPG):
        t = v[j] > b1
        b1, i1, w1 = jnp.where(t, v[j], b1), jnp.where(t, j, i1), jnp.where(t, w[j], w1)
    b2, i2, w2 = jnp.full_like(b1, -jnp.inf), jnp.zeros_like(gidx), jnp.zeros_like(w1)
    for j in range(EPG):
        t = jnp.logical_and(i1 != j, v[j] > b2)
        b2, i2, w2 = jnp.where(t, v[j], b2), jnp.where(t, j, i2), jnp.where(t, w[j], w2)
    den = w1 + w2
    e1 = (gidx * EPG + i1).astype(F32)
    e2 = (gidx * EPG + i2).astype(F32)
    return jnp.where(lane == 0, w1 / den, jnp.where(lane == 1, w2 / den,
                     jnp.where(lane == 2, e1, jnp.where(lane == 3, e2, 0.0))))


def _merge_kernel(na_ref, hf_ref, hb_ref, og_ref, sc_ref, hy_ref, gain_ref, wout_ref, x_ref, m2_ref, m3_ref, m4_ref,
                  gffn_ref, rwh_ref, rwl_ref, rb_ref, bd64_ref, bd128_ref, xo_ref, hp_ref, s_ref):
    gain = gain_ref[...]
    y_ml = jax.nn.sigmoid(og_ref[0].astype(F32)) * (hf_ref[0].astype(F32) + hb_ref[0].astype(F32))
    parts = ((na_ref[0].astype(F32), bd64_ref), (y_ml, bd128_ref), (sc_ref[0].astype(F32), bd64_ref),
             (hy_ref[0], bd64_ref))
    acc = None
    for n, (y, bd) in enumerate(parts):
        yn = _group_rms(y, bd[...], gain[:, n * MIX_W:(n + 1) * MIX_W]).astype(BF16)
        t = _dot(yn, wout_ref[n * MIX_W:(n + 1) * MIX_W, :])
        acc = t if acc is None else acc + t
    x = x_ref[0] + m2_ref[0] * acc
    xo_ref[0] = x
    ms = jnp.mean(x * x, axis=-1, keepdims=True)
    h2 = (x * lax.rsqrt(ms + NORM_EPS) * gffn_ref[...]) * (1.0 + m4_ref[0]) + m3_ref[0]
    hi = h2.astype(BF16)
    lo = (h2 - hi.astype(F32)).astype(BF16)
    logits = _dot(hi, rwh_ref[...]) + _dot(hi, rwl_ref[...]) + _dot(lo, rwh_ref[...])
    s_ref[0] = _route_tile(jax.nn.sigmoid(logits), rb_ref[...])
    half = D_MODEL // 2
    hf32 = hi.astype(F32)
    hp_ref[0] = _pack(hf32[:, :half], hf32[:, half:])


def _merge(y_na, hf, hb, p, y_sc, y_hy, out_gain, w_out, x, m2, m3, m4, g_ffn, rwh, rwl, rb):
    B, L, D = x.shape
    tm = min(256, L)
    bd64 = _block_diag_mean(MIX_W, 64)
    bd128 = _block_diag_mean(MIX_W, 128)
    row = lambda w: pl.BlockSpec((1, tm, w), lambda b, i: (b, i, 0))
    mod = pl.BlockSpec((1, 1, D), lambda b, i: (b, 0, 0))
    full = lambda a: pl.BlockSpec(a.shape, lambda b, i: (0,) * a.ndim)
    return pl.pallas_call(
        _merge_kernel,
        grid=(B, L // tm),
        in_specs=[row(MIX_W), row(MIX_W), row(MIX_W),
                  pl.BlockSpec((1, tm, MIX_W), lambda b, i: (b, i, CB_ML_O)),
                  row(MIX_W), row(MIX_W), full(out_gain), full(w_out), row(D), mod, mod, mod,
                  full(g_ffn), full(rwh), full(rwl), full(rb), full(bd64), full(bd128)],
        out_specs=[row(D), row(D // 2), row(LANES)],
        out_shape=[jax.ShapeDtypeStruct((B, L, D), F32), jax.ShapeDtypeStruct((B, L, D // 2), U32),
                   jax.ShapeDtypeStruct((B, L, LANES), F32)],
        compiler_params=_cp(("arbitrary", "arbitrary")),
        name="merge",
    )(y_na, hf, hb, p, y_sc, y_hy, out_gain, w_out, x, m2, m3, m4, g_ffn, rwh, rwl, rb, bd64, bd128)


def _plan_kernel(r_ref, tri_ref, rank_ref, cnt_ref, carry):
    @pl.when(pl.program_id(0) == 0)
    def _():
        carry[...] = jnp.zeros_like(carry)

    r = r_ref[...]
    lane = lax.broadcasted_iota(jnp.int32, r.shape, 1)
    o0 = lane == r[:, 2:3].astype(jnp.int32)
    o1 = lane == r[:, 3:4].astype(jnp.int32)
    o = jnp.where(jnp.logical_or(o0, o1), 1.0, 0.0)
    tot = _dot(tri_ref[...], o.astype(BF16)) + carry[0:1, :]
    r0 = jnp.sum(jnp.where(o0, tot, 0.0), axis=-1, keepdims=True)
    r1 = jnp.sum(jnp.where(o1, tot, 0.0), axis=-1, keepdims=True)
    rank_ref[...] = jnp.where(lane == 0, r0, jnp.where(lane == 1, r1, 0.0))
    carry[...] = carry[...] + jnp.sum(o, axis=0, keepdims=True)
    cnt_ref[...] = carry[...]


def _plan(route):
    T = route.shape[0]
    tm = 512
    tri = jnp.asarray(np.tril(np.ones((tm, tm)), -1), BF16)
    rank, cnt = pl.pallas_call(
        _plan_kernel,
        grid=(T // tm,),
        in_specs=[pl.BlockSpec((tm, LANES), lambda i: (i, 0)), pl.BlockSpec((tm, tm), lambda i: (0, 0))],
        out_specs=[pl.BlockSpec((tm, LANES), lambda i: (i, 0)), pl.BlockSpec((SUBLANES, LANES), lambda i: (0, 0))],
        out_shape=[jax.ShapeDtypeStruct((T, LANES), F32), jax.ShapeDtypeStruct((SUBLANES, LANES), F32)],
        scratch_shapes=[pltpu.VMEM((SUBLANES, LANES), F32)],
        compiler_params=_cp(("arbitrary",)),
        name="moe_plan",
    )(route, tri)
    A = T * TOP_K
    counts = cnt[0, :N_EXPERTS].astype(jnp.int32)
    padded = (counts + MOE_BLOCK - 1) // MOE_BLOCK * MOE_BLOCK
    pend = jnp.cumsum(padded)
    pstart = pend - padded
    expert = route[:, 2:2 + TOP_K].astype(jnp.int32)
    onehot = expert[:, :, None] == jnp.arange(N_EXPERTS, dtype=jnp.int32)[None, None, :]
    base = jnp.sum(jnp.where(onehot, pstart[None, None, :], 0), axis=-1)
    dest = base + rank[:, :TOP_K].astype(jnp.int32)
    n_rows = (A + MOE_BLOCK - 1) // MOE_BLOCK * MOE_BLOCK + N_EXPERTS * MOE_BLOCK
    n_blocks = n_rows // MOE_BLOCK
    starts = jnp.arange(n_blocks, dtype=jnp.int32) * MOE_BLOCK
    block_e = jnp.minimum(jnp.sum((pend[None, :] <= starts[:, None]).astype(jnp.int32), axis=1), N_EXPERTS - 1)
    n_used = (pend[-1] // MOE_BLOCK).astype(jnp.int32).reshape(1)
    return dest, block_e.astype(jnp.int32), n_used, n_rows


def _scatter_kernel(dest_ref, h_ref, xs_in_ref, xs_ref, sem, *, tm):
    del xs_in_ref

    def copy(t, d):
        return pltpu.make_async_copy(h_ref.at[pl.ds(t, 1), :], xs_ref.at[pl.ds(d, 1), :], sem)

    def issue(t, c):
        for k in range(TOP_K):
            copy(t, dest_ref[0, 0, TOP_K * t + k]).start()
        return c

    def drain(t, c):
        for k in range(TOP_K):
            copy(t, dest_ref[0, 0, TOP_K * t + k]).wait()
        return c

    lax.fori_loop(0, tm, issue, 0)
    lax.fori_loop(0, tm, drain, 0)


def _scatter(h2p, dest, xs):
    T, W = h2p.shape
    tm = 256
    return pl.pallas_call(
        functools.partial(_scatter_kernel, tm=tm),
        grid=(T // tm,),
        in_specs=[pl.BlockSpec((1, 1, TOP_K * tm), lambda i: (i, 0, 0), memory_space=pltpu.SMEM),
                  pl.BlockSpec((tm, W), lambda i: (i, 0)),
                  pl.BlockSpec(memory_space=pl.ANY)],
        out_specs=pl.BlockSpec(memory_space=pl.ANY),
        out_shape=jax.ShapeDtypeStruct(xs.shape, xs.dtype),
        scratch_shapes=[pltpu.SemaphoreType.DMA(())],
        input_output_aliases={2: 0},
        compiler_params=_cp(("arbitrary",)),
        name="moe_scatter",
    )(dest.reshape(T // tm, 1, TOP_K * tm), h2p, xs)


def _ffn_kernel(be_ref, nu_ref, x_ref, wg_ref, wu_ref, wd_ref, y_ref):
    del be_ref
    used = pl.program_id(0) < nu_ref[0]
    half = D_MODEL // 2

    @pl.when(used)
    def _():
        lo, hi = _unpack(x_ref[...])
        lo, hi = lo.astype(BF16), hi.astype(BF16)
        g = _dot(lo, wg_ref[0, :half, :]) + _dot(hi, wg_ref[0, half:, :])
        u = _dot(lo, wu_ref[0, :half, :]) + _dot(hi, wu_ref[0, half:, :])
        h = (g * jax.nn.sigmoid(g) * u).astype(BF16)
        y = _dot(h, wd_ref[0])
        y_ref[...] = _pack(y[:, :half], y[:, half:])

    @pl.when(jnp.logical_not(used))
    def _():
        y_ref[...] = jnp.zeros_like(y_ref)


def _ffn(xs, block_e, n_used, wg, wu, wd):
    n_rows, W = xs.shape
    nb = n_rows // MOE_BLOCK
    grid_spec = pltpu.PrefetchScalarGridSpec(
        num_scalar_prefetch=2,
        grid=(nb,),
        in_specs=[pl.BlockSpec((MOE_BLOCK, W), lambda i, be, nu: (i, 0)),
                  pl.BlockSpec((1, D_MODEL, EXPERT_FF), lambda i, be, nu: (be[i], 0, 0)),
                  pl.BlockSpec((1, D_MODEL, EXPERT_FF), lambda i, be, nu: (be[i], 0, 0)),
                  pl.BlockSpec((1, EXPERT_FF, D_MODEL), lambda i, be, nu: (be[i], 0, 0))],
        out_specs=pl.BlockSpec((MOE_BLOCK, W), lambda i, be, nu: (i, 0)),
    )
    return pl.pallas_call(
        _ffn_kernel,
        grid_spec=grid_spec,
        out_shape=jax.ShapeDtypeStruct((n_rows, W), U32),
        compiler_params=_cp(("arbitrary",)),
        name="moe_ffn",
    )(block_e, n_used, xs, wg, wu, wd)


def _combine_kernel(dest_ref, yb_ref, g_ref, x_ref, m5_ref, o_ref, buf, sem, *, tm):
    def copy(t, k, d):
        return pltpu.make_async_copy(yb_ref.at[pl.ds(d, 1), :], buf.at[k, pl.ds(t, 1), :], sem)

    def issue(t, c):
        for k in range(TOP_K):
            copy(t, k, dest_ref[0, 0, TOP_K * t + k]).start()
        return c

    def drain(t, c):
        for k in range(TOP_K):
            copy(t, k, dest_ref[0, 0, TOP_K * t + k]).wait()
        return c

    lax.fori_loop(0, tm, issue, 0)
    lax.fori_loop(0, tm, drain, 0)
    g = g_ref[...]
    a_lo, a_hi = _unpack(buf[0])
    b_lo, b_hi = _unpack(buf[1])
    g0, g1 = g[:, 0:1], g[:, 1:2]
    half = D_MODEL // 2
    m5 = m5_ref[0]
    o_ref[:, :half] = x_ref[:, :half] + m5[:, :half] * (g0 * a_lo + g1 * b_lo)
    o_ref[:, half:] = x_ref[:, half:] + m5[:, half:] * (g0 * a_hi + g1 * b_hi)


def _combine(yb, dest, gp, x2d, m5, rows_per_batch):
    T, D = x2d.shape
    tm = min(256, rows_per_batch)
    tpb = rows_per_batch // tm
    return pl.pallas_call(
        functools.partial(_combine_kernel, tm=tm),
        grid=(T // tm,),
        in_specs=[pl.BlockSpec((1, 1, TOP_K * tm), lambda i: (i, 0, 0), memory_space=pltpu.SMEM),
                  pl.BlockSpec(memory_space=pl.ANY),
                  pl.BlockSpec((tm, LANES), lambda i: (i, 0)),
                  pl.BlockSpec((tm, D), lambda i: (i, 0)),
                  pl.BlockSpec((1, 1, D), lambda i: (i // tpb, 0, 0))],
        out_specs=pl.BlockSpec((tm, D), lambda i: (i, 0)),
        out_shape=jax.ShapeDtypeStruct((T, D), F32),
        scratch_shapes=[pltpu.VMEM((TOP_K, tm, D // 2), U32), pltpu.SemaphoreType.DMA(())],
        compiler_params=_cp(("arbitrary",)),
        name="moe_combine",
    )(dest.reshape(T // tm, 1, TOP_K * tm), yb, gp, x2d, m5)


def _moe(streams, wg, wu, wd):
    route = jnp.concatenate([s[2].reshape(-1, LANES) for s in streams], axis=0)
    dest, block_e, n_used, n_rows = _plan(route)
    xs = jnp.zeros((n_rows, D_MODEL // 2), U32)
    off = 0
    for s in streams:
        n = s[1].shape[0] * s[1].shape[1]
        xs = _scatter(s[1].reshape(n, D_MODEL // 2), dest[off:off + n], xs)
        off += n
    yb = _ffn(xs, block_e, n_used, wg, wu, wd)
    outs, off = [], 0
    for x, _, _, m5 in streams:
        B, L, D = x.shape
        n = B * L
        o = _combine(yb, dest[off:off + n], route[off:off + n], x.reshape(n, D), m5, L)
        outs.append(o.reshape(B, L, D))
        off += n
    return outs


def _layer(x, xc, mod, modc, need_ctx, norm_mix, norm_ffn, w_in, gate_b, w_out, out_norm, na_q, na_k, na_rpb,
           ml_cw, ml_cb, sc_w, hy_cw, hy_cb, hy_w1, hy_b1, hy_w2, hy_b2, hy_w3, hy_freq, hy_bias,
           router_w, router_b, wg, wu, wd, rope_tabs):
    B, L, D = x.shape
    w_main = jnp.concatenate([w_in[:, :GATE_COL0], w_in[:, GATE_COL0 + ML_GATES:]], axis=1).astype(BF16)
    w_gate = jnp.zeros((D, LANES), BF16).at[:, :ML_GATES].set(w_in[:, GATE_COL0:GATE_COL0 + ML_GATES].astype(BF16))
    b_gate = jnp.zeros((1, LANES), F32).at[0, :ML_GATES].set(gate_b.astype(F32))
    gain_mix = norm_mix.reshape(1, D).astype(F32)

    p, g = _inproj(x, mod[0], mod[1], gain_mix, w_main, w_gate, b_gate)
    pc, gc = _inproj(xc, modc[0], modc[1], gain_mix, w_main, w_gate, b_gate)

    mlq, mlk, y_sc, hv, hx1, hx2 = _prep(p, ml_cw, ml_cb, sc_w, hy_cw, hy_cb, rope_tabs)
    mlqc, mlkc, yc_sc, hvc, hx1c, hx2c = _prep(pc, ml_cw, ml_cb, sc_w, hy_cw, hy_cb, None)

    y_na = _na(p, pc, na_rpb, na_q, na_k)
    hf, hb, hcf, hcb = _mlstm(mlq, mlk, p, g, mlqc, mlkc, pc, gc)
    y_hy = _hyena(hv, hx1, hx2, hy_w1, hy_b1, hy_w2, hy_b2, hy_w3, hy_freq, hy_bias)

    w_out_b = w_out.astype(BF16)
    out_gain = out_norm.reshape(1, D).astype(F32)
    g_ffn = norm_ffn.reshape(1, D).astype(F32)
    rw = _route_lanes(router_w)
    rwh = rw.astype(BF16)
    rwl = (rw - rwh.astype(F32)).astype(BF16)
    rb = _route_lanes(router_b).reshape(1, LANES)

    x1, h2p, sc = _merge(y_na, hf, hb, p, y_sc, y_hy, out_gain, w_out_b, x, mod[2], mod[3], mod[4], g_ffn, rwh, rwl,
                         rb)
    streams = [(x1, h2p, sc, mod[5])]
    if need_ctx:
        yc_na = _ctx_attn(pc, na_q, na_k)
        yc_hy = _hyena(hvc, hx1c, hx2c, hy_w1, hy_b1, hy_w2, hy_b2, hy_w3, hy_freq, hy_bias)
        xc1, h2pc, scc = _merge(yc_na, hcf, hcb, pc, yc_sc, yc_hy, out_gain, w_out_b, xc, modc[2], modc[3], modc[4],
                                g_ffn, rwh, rwl, rb)
        streams.append((xc1, h2pc, scc, modc[5]))
    outs = _moe(streams, wg.astype(BF16), wu.astype(BF16), wd.astype(BF16))
    return outs[0], (outs[1] if need_ctx else None)


def kernel(x, c, ctx, c_ctx, w_ada, b_ada, norm_mix, norm_ffn, w_in, mlstm_gate_bias, w_out, out_norm, na_q_norm, na_k_norm, na_rpb, mlstm_conv_w, mlstm_conv_b, sconv_w, hyena_conv_w, hyena_conv_b, hyena_f_w1, hyena_f_b1, hyena_f_w2, hyena_f_b2, hyena_f_w3, hyena_f_freq, hyena_bias, router_w, router_bias, moe_w_gate, moe_w_up, moe_w_down):
    B, L, D = x.shape
    depth = w_ada.shape[0]
    xc = ctx
    s = jnp.zeros((SUBLANES, D), F32)
    s = s.at[:B].set(jax.nn.silu(c)).at[B].set(jax.nn.silu(c_ctx))
    rope_tabs = _rope_tables(L)
    for l in range(depth):
        need_ctx = l < depth - 1
        m = _ada(s, w_ada[l], b_ada[l])
        mod = [m[:B, n * D:(n + 1) * D].reshape(B, 1, D) for n in range(6)]
        modc = [jnp.broadcast_to(m[B, n * D:(n + 1) * D].reshape(1, 1, D), (B, 1, D)) for n in range(6)]
        x, xc = _layer(x, xc, mod, modc, need_ctx, norm_mix[l], norm_ffn[l], w_in[l], mlstm_gate_bias[l], w_out[l],
                       out_norm[l], na_q_norm[l], na_k_norm[l], na_rpb[l], mlstm_conv_w[l], mlstm_conv_b[l],
                       sconv_w[l], hyena_conv_w[l], hyena_conv_b[l], hyena_f_w1[l], hyena_f_b1[l], hyena_f_w2[l],
                       hyena_f_b2[l], hyena_f_w3[l], hyena_f_freq[l], hyena_bias[l], router_w, router_bias,
                       moe_w_gate[l], moe_w_up[l], moe_w_down[l], rope_tabs)
    return x
```

```python
import functools
import math

import numpy as np
import jax
import jax.numpy as jnp
from jax import lax
from jax.experimental import pallas as pl
from jax.experimental.pallas import tpu as pltpu

F32 = jnp.float32
BF16 = jnp.bfloat16
U32 = jnp.uint32

D_MODEL = 2048
GRID_W = 64
NA_HEADS, NA_HD, NA_ROWS, NA_COLS = 8, 64, 8, 16
ML_HEADS, ML_HD = 4, 128
ML_GATES = 4 * ML_HEADS
ROPE_BASE = 10000.0
MIX_W = 512
HY_ORDER, HY_BANDS, HY_FFN = 2, 8, 64
HY_EMB = 1 + 2 * HY_BANDS
HY_FAST_DECAY, HY_SLOW_DECAY, HY_TARGET = 0.3, 1.5, 0.01
N_EXPERTS, N_GROUPS, TOP_K, EXPERT_FF = 32, 8, 2, 1408
EPG = N_EXPERTS // N_GROUPS
NORM_EPS = 1e-6
GATE_COL0 = 3 * MIX_W + 4 * MIX_W
P_COLS = 13 * MIX_W
CB_NA_Q, CB_NA_K, CB_NA_V = 0, 1, 2
CB_ML_Q, CB_ML_K, CB_ML_V, CB_ML_O = 3, 4, 5, 6
CB_SC_B, CB_SC_C, CB_SC_X = 7, 8, 9
CB_HY_V, CB_HY_X1, CB_HY_X2 = 10, 11, 12

LANES = 128
SUBLANES = 8
BF16_ROWS = 16
VMEM_LIMIT = 56 * 1024 * 1024

ML_CHUNK = 256
FFT_N2 = 256
FFT_JB = SUBLANES
MOE_BLOCK = 256
DMA_UNROLL = 8
NEG = -1e30


def _cp(sem, vmem=VMEM_LIMIT):
    return pltpu.CompilerParams(dimension_semantics=sem, vmem_limit_bytes=vmem)


def _dot(a, b):
    return jnp.dot(a, b, preferred_element_type=F32)


def _dot_nt(a, b):
    return lax.dot_general(a, b, (((1,), (1,)), ((), ())), preferred_element_type=F32)


def _dot_hi(a, b):
    return jnp.dot(a, b, preferred_element_type=F32, precision=lax.Precision.HIGHEST)


def _pack(lo, hi):
    a = lax.bitcast_convert_type(lo.astype(BF16).astype(F32), U32) >> 16
    b = lax.bitcast_convert_type(hi.astype(BF16).astype(F32), U32) & jnp.uint32(0xFFFF0000)
    return a | b


def _unpack(w):
    lo = lax.bitcast_convert_type(w << 16, F32)
    hi = lax.bitcast_convert_type(w & jnp.uint32(0xFFFF0000), F32)
    return lo, hi


def _block_diag_mean(width, group):
    m = np.kron(np.eye(width // group), np.full((group, group), 1.0 / group))
    return jnp.asarray(m, BF16)


def _group_rms(y, bd, gain):
    ms = _dot((y * y).astype(BF16), bd)
    return y * lax.rsqrt(ms + NORM_EPS) * gain


def _ada_kernel(s_ref, w_ref, b_ref, o_ref):
    o_ref[...] = _dot_hi(s_ref[...], w_ref[...]) + b_ref[...]


def _ada(s, w, b):
    tn = 1536
    n = w.shape[1]
    return pl.pallas_call(
        _ada_kernel,
        grid=(n // tn,),
        in_specs=[pl.BlockSpec((SUBLANES, D_MODEL), lambda j: (0, 0)),
                  pl.BlockSpec((D_MODEL, tn), lambda j: (0, j)),
                  pl.BlockSpec((1, tn), lambda j: (0, j))],
        out_specs=pl.BlockSpec((SUBLANES, tn), lambda j: (0, j)),
        out_shape=jax.ShapeDtypeStruct((SUBLANES, n), F32),
        compiler_params=_cp(("arbitrary",)),
        name="ada",
    )(s, w, b.reshape(1, n))


def _inproj_kernel(x_ref, sh_ref, sc_ref, gain_ref, w_ref, wg_ref, bg_ref, p_ref, g_ref, h_scr):
    @pl.when(pl.program_id(2) == 0)
    def _():
        x = x_ref[0]
        ms = jnp.mean(x * x, axis=-1, keepdims=True)
        y = x * lax.rsqrt(ms + NORM_EPS) * gain_ref[...]
        h = (y * (1.0 + sc_ref[0]) + sh_ref[0]).astype(BF16)
        h_scr[...] = h
        g_ref[0] = _dot(h, wg_ref[...]) + bg_ref[...]

    p_ref[0] = _dot(h_scr[...], w_ref[...]).astype(BF16)


def _inproj(x, shift, scale, gain, w_main, w_gate, b_gate):
    B, L, D = x.shape
    tm = min(512, L)
    tn = P_COLS // 4
    return pl.pallas_call(
        _inproj_kernel,
        grid=(B, L // tm, P_COLS // tn),
        in_specs=[pl.BlockSpec((1, tm, D), lambda b, i, j: (b, i, 0)),
                  pl.BlockSpec((1, 1, D), lambda b, i, j: (b, 0, 0)),
                  pl.BlockSpec((1, 1, D), lambda b, i, j: (b, 0, 0)),
                  pl.BlockSpec((1, D), lambda b, i, j: (0, 0)),
                  pl.BlockSpec((D, tn), lambda b, i, j: (0, j)),
                  pl.BlockSpec((D, LANES), lambda b, i, j: (0, 0)),
                  pl.BlockSpec((1, LANES), lambda b, i, j: (0, 0))],
        out_specs=[pl.BlockSpec((1, tm, tn), lambda b, i, j: (b, i, j)),
                   pl.BlockSpec((1, tm, LANES), lambda b, i, j: (b, i, 0))],
        out_shape=[jax.ShapeDtypeStruct((B, L, P_COLS), BF16),
                   jax.ShapeDtypeStruct((B, L, LANES), F32)],
        scratch_shapes=[pltpu.VMEM((tm, D), BF16)],
        compiler_params=_cp(("arbitrary", "arbitrary", "arbitrary")),
        name="inproj",
    )(x, shift, scale, gain, w_main, w_gate, b_gate)


_CONV_BLOCKS = (CB_ML_Q, CB_ML_K, CB_SC_C, CB_SC_X, CB_HY_V, CB_HY_X1, CB_HY_X2)


def _prep_kernel(*refs, rope, tm):
    n_conv = len(_CONV_BLOCKS)
    main = refs[0:n_conv]
    prev = refs[n_conv:2 * n_conv]
    nxt = refs[2 * n_conv:3 * n_conv]
    k = 3 * n_conv
    scb_ref, mlw_ref, mlb_ref, scw_ref, hyw_ref, hyb_ref = refs[k:k + 6]
    k += 6
    if rope:
        cos_ref, sin_ref = refs[k:k + 2]
        k += 2
    q_ref, k_ref, sc_ref, hv_ref, hx1_ref, hx2_ref = refs[k:k + 6]

    i = pl.program_id(1)
    first = i == 0
    last = i == pl.num_programs(1) - 1
    rid = lax.broadcasted_iota(jnp.int32, (tm, MIX_W), 0)

    def rows(idx):
        xm = main[idx][0].astype(F32)
        xp = prev[idx][0].astype(F32)[BF16_ROWS - 1:BF16_ROWS, :]
        xn = nxt[idx][0].astype(F32)[0:1, :]
        return xm, xp, xn

    def conv(xm, xp, xn, w, b):
        xp = jnp.where(first, 0.0, xp)
        xn = jnp.where(last, 0.0, xn)
        up = jnp.where(rid == 0, xp, pltpu.roll(xm, 1, axis=0))
        dn = jnp.where(rid == tm - 1, xn, pltpu.roll(xm, tm - 1, axis=0))
        y = up * w[0:1] + xm * w[1:2] + dn * w[2:3]
        return y if b is None else y + b

    def silu(v):
        return v * jax.nn.sigmoid(v)

    def rot(t):
        if not rope:
            return t
        lane = lax.broadcasted_iota(jnp.int32, (tm, LANES), 1)
        first_half = (lane % 64) < 32
        cos, sin = cos_ref[...], sin_ref[...]
        outs = []
        for h in range(ML_HEADS):
            th = t[:, h * ML_HD:(h + 1) * ML_HD]
            partner = jnp.where(first_half, pltpu.roll(th, LANES - 32, axis=1), pltpu.roll(th, 32, axis=1))
            outs.append(th * cos + partner * sin)
        return jnp.concatenate(outs, axis=1)

    mlw, mlb = mlw_ref[...], mlb_ref[...]
    q = silu(conv(*rows(0), mlw[:, :MIX_W], mlb[:, :MIX_W]))
    kk = silu(conv(*rows(1), mlw[:, MIX_W:], mlb[:, MIX_W:]))
    q_ref[0] = rot(q).astype(BF16)
    k_ref[0] = (rot(kk) * ML_HD ** -0.5).astype(BF16)

    cm, cp_, cn = rows(2)
    xm, xp, xn = rows(3)
    sc = conv(cm * xm, cp_ * xp, cn * xn, scw_ref[...], None)
    sc_ref[0] = (scb_ref[0].astype(F32) * sc).astype(BF16)

    hyw, hyb = hyw_ref[...], hyb_ref[...]
    for n, o_ref in enumerate((hv_ref, hx1_ref, hx2_ref)):
        o_ref[0] = conv(*rows(4 + n), hyw[:, n * MIX_W:(n + 1) * MIX_W], hyb[:, n * MIX_W:(n + 1) * MIX_W])


def _prep(p, ml_w, ml_b, sc_w, hy_w, hy_b, rope_tabs):
    B, L, _ = p.shape
    tm = 256
    nt = L // tm
    hb = tm // BF16_ROWS
    n_halo = L // BF16_ROWS
    rope = rope_tabs is not None

    in_specs, args = [], []
    for cb in _CONV_BLOCKS:
        in_specs.append(pl.BlockSpec((1, tm, MIX_W), lambda b, i, cb=cb: (b, i, cb)))
        args.append(p)
    for cb in _CONV_BLOCKS:
        in_specs.append(pl.BlockSpec((1, BF16_ROWS, MIX_W), lambda b, i, cb=cb: (b, jnp.maximum(i * hb - 1, 0), cb)))
        args.append(p)
    for cb in _CONV_BLOCKS:
        in_specs.append(pl.BlockSpec((1, BF16_ROWS, MIX_W),
                                     lambda b, i, cb=cb: (b, jnp.minimum((i + 1) * hb, n_halo - 1), cb)))
        args.append(p)
    in_specs.append(pl.BlockSpec((1, tm, MIX_W), lambda b, i: (b, i, CB_SC_B)))
    args.append(p)
    for w in (ml_w, ml_b.reshape(1, -1), sc_w, hy_w, hy_b.reshape(1, -1)):
        in_specs.append(pl.BlockSpec(w.shape, lambda b, i: (0, 0)))
        args.append(w)
    if rope:
        for t in rope_tabs:
            in_specs.append(pl.BlockSpec((tm, LANES), lambda b, i: (i, 0)))
            args.append(t)
    blk = pl.BlockSpec((1, tm, MIX_W), lambda b, i: (b, i, 0))
    return pl.pallas_call(
        functools.partial(_prep_kernel, rope=rope, tm=tm),
        grid=(B, nt),
        in_specs=in_specs,
        out_specs=[blk] * 6,
        out_shape=[jax.ShapeDtypeStruct((B, L, MIX_W), BF16)] * 3 + [jax.ShapeDtypeStruct((B, L, MIX_W), F32)] * 3,
        compiler_params=_cp(("arbitrary", "arbitrary")),
        name="prep",
    )(*args)


def _rope_tables(L):
    quarter = ML_HD // 4
    pos = jnp.arange(L, dtype=jnp.int32)
    inv = ROPE_BASE ** (-jnp.arange(quarter, dtype=F32) / quarter)
    ar = (pos // GRID_W).astype(F32)[:, None] * inv[None, :]
    ac = (pos % GRID_W).astype(F32)[:, None] * inv[None, :]
    cos = jnp.concatenate([jnp.cos(ar), jnp.cos(ar), jnp.cos(ac), jnp.cos(ac)], axis=1)
    sin = jnp.concatenate([-jnp.sin(ar), jnp.sin(ar), -jnp.sin(ac), jnp.sin(ac)], axis=1)
    return cos, sin


NA_GROUP = 8
NA_Q = NA_GROUP * GRID_W
NA_KEYS = 2 * NA_Q


def _head_norm(x_bf16, bd, gain):
    x = x_bf16.astype(F32)
    return _group_rms(x, bd, gain)


def _pair_attention(qn, keys, vals, biases):
    lane = lax.broadcasted_iota(jnp.int32, qn.shape, 1)
    outs = []
    for half in range(2):
        sel = (lane < NA_HD) if half == 0 else (lane >= NA_HD)
        qh = jnp.where(sel, qn, jnp.zeros_like(qn))
        scores = []
        for kk, bias in zip(keys, biases[half]):
            s = _dot_nt(qh, kk)
            scores.append(s if bias is None else s + bias)
        m = functools.reduce(jnp.maximum, [jnp.max(s, axis=-1, keepdims=True) for s in scores])
        es = [jnp.exp(s - m) for s in scores]
        den = functools.reduce(jnp.add, [jnp.sum(e, axis=-1, keepdims=True) for e in es])
        o = functools.reduce(jnp.add, [_dot(e.astype(BF16), v) for e, v in zip(es, vals)])
        outs.append(o / den)
    return jnp.where(lane < NA_HD, outs[0], outs[1])


def _na_kernel(q_ref, kp_ref, kc_ref, kn_ref, vp_ref, vc_ref, vn_ref, ck_ref, cv_ref, tab_ref,
               qg_ref, kg_ref, bd_ref, o_ref):
    bd = bd_ref[...]
    half_q = NA_Q // 2
    qn = (_head_norm(q_ref[0], bd, qg_ref[...]) * NA_HD ** -0.5).astype(BF16)
    k_raw = jnp.concatenate([kp_ref[0][half_q:], kc_ref[0], kn_ref[0][:half_q]], axis=0)
    kun = _head_norm(k_raw, bd, kg_ref[...]).astype(BF16)
    vun = jnp.concatenate([vp_ref[0][half_q:], vc_ref[0], vn_ref[0][:half_q]], axis=0)
    ckn = _head_norm(ck_ref[0], bd, kg_ref[...]).astype(BF16)
    o = _pair_attention(qn, [kun, ckn], [vun, cv_ref[0]], [[tab_ref[0, 0], None], [tab_ref[0, 1], None]])
    o_ref[0] = o.astype(BF16)


def _na_table(rpb, rows):
    wr = NA_ROWS
    qc = np.arange(GRID_W)[:, None]
    kc = np.arange(GRID_W)[None, :]
    col0 = np.clip(qc - NA_COLS // 2, 0, GRID_W - NA_COLS)
    col_ok = (kc >= col0) & (kc < col0 + NA_COLS)
    nd, ndc = 2 * wr - 1, 2 * NA_COLS - 1
    dc = kc - qc + NA_COLS - 1
    onehot = (np.arange(ndc)[:, None, None] == dc[None]) & col_ok[None]
    cb = jnp.dot(rpb.reshape(NA_HEADS * nd, ndc), jnp.asarray(onehot.reshape(ndc, -1), F32),
                 precision=lax.Precision.HIGHEST).reshape(NA_HEADS, nd, GRID_W, GRID_W)
    cb = jnp.where(col_ok[None, None], cb, NEG)
    cb = jnp.concatenate([cb, jnp.full((NA_HEADS, 1, GRID_W, GRID_W), NEG, F32)], axis=1)
    d_idx = np.full((3, NA_GROUP, 2 * NA_GROUP), nd, np.int32)
    for kind, R in enumerate((0, NA_GROUP, rows - NA_GROUP)):
        for rr in range(NA_GROUP):
            r = R + rr
            r0 = int(np.clip(r - wr // 2, 0, rows - wr))
            for u in range(2 * NA_GROUP):
                key_row = R - NA_GROUP // 2 + u
                if r0 <= key_row <= r0 + wr - 1:
                    d_idx[kind, rr, u] = key_row - r + wr - 1
    t = jnp.take(cb, jnp.asarray(d_idx.reshape(-1)), axis=1)
    t = t.reshape(NA_HEADS, 3, NA_GROUP, 2 * NA_GROUP, GRID_W, GRID_W)
    return jnp.transpose(t, (1, 0, 2, 4, 3, 5)).reshape(3, NA_HEADS, NA_Q, NA_KEYS)


def _na(p, pc, rpb, q_gain, k_gain):
    B, L, _ = p.shape
    ctx = pc.shape[1]
    ng = L // NA_Q
    rows = L // GRID_W
    tab = _na_table(rpb.astype(F32), rows)
    qg = jnp.tile(q_gain.astype(F32), 2).reshape(1, LANES)
    kg = jnp.tile(k_gain.astype(F32), 2).reshape(1, LANES)
    bd = _block_diag_mean(LANES, NA_HD)
    npair = NA_HEADS // 2
    cpb = MIX_W // LANES

    def kind(i):
        return jnp.where(i == 0, 0, jnp.where(i == ng - 1, 2, 1))

    def blk(cb, which):
        def idx(hp, b, i):
            if which == 0:
                r = i
            elif which < 0:
                r = jnp.maximum(i - 1, 0)
            else:
                r = jnp.minimum(i + 1, ng - 1)
            return (b, r, cb * cpb + hp)
        return pl.BlockSpec((1, NA_Q, LANES), idx)

    def cblk(cb):
        return pl.BlockSpec((1, ctx, LANES), lambda hp, b, i: (b, 0, cb * cpb + hp))

    small = lambda shape: pl.BlockSpec(shape, lambda hp, b, i: (0, 0))
    return pl.pallas_call(
        _na_kernel,
        grid=(npair, B, ng),
        in_specs=[blk(CB_NA_Q, 0), blk(CB_NA_K, -1), blk(CB_NA_K, 0), blk(CB_NA_K, 1),
                  blk(CB_NA_V, -1), blk(CB_NA_V, 0), blk(CB_NA_V, 1), cblk(CB_NA_K), cblk(CB_NA_V),
                  pl.BlockSpec((1, 2, NA_Q, NA_KEYS), lambda hp, b, i: (kind(i), hp, 0, 0)),
                  small((1, LANES)), small((1, LANES)), small((LANES, LANES))],
        out_specs=pl.BlockSpec((1, NA_Q, LANES), lambda hp, b, i: (b, i, hp)),
        out_shape=jax.ShapeDtypeStruct((B, L, MIX_W), BF16),
        compiler_params=_cp(("arbitrary", "arbitrary", "arbitrary")),
        name="na",
    )(p, p, p, p, p, p, p, pc, pc, tab, qg, kg, bd)


def _ctx_attn_kernel(q_ref, k_ref, v_ref, qg_ref, kg_ref, bd_ref, o_ref):
    bd = bd_ref[...]
    qn = (_head_norm(q_ref[0], bd, qg_ref[...]) * NA_HD ** -0.5).astype(BF16)
    kn = _head_norm(k_ref[0], bd, kg_ref[...]).astype(BF16)
    o = _pair_attention(qn, [kn], [v_ref[0]], [[None], [None]])
    o_ref[0] = o.astype(BF16)


def _ctx_attn(pc, q_gain, k_gain):
    B, ctx, _ = pc.shape
    qg = jnp.tile(q_gain.astype(F32), 2).reshape(1, LANES)
    kg = jnp.tile(k_gain.astype(F32), 2).reshape(1, LANES)
    bd = _block_diag_mean(LANES, NA_HD)
    cpb = MIX_W // LANES
    blk = lambda cb: pl.BlockSpec((1, ctx, LANES), lambda hp, b: (b, 0, cb * cpb + hp))
    small = lambda shape: pl.BlockSpec(shape, lambda hp, b: (0, 0))
    return pl.pallas_call(
        _ctx_attn_kernel,
        grid=(NA_HEADS // 2, B),
        in_specs=[blk(CB_NA_Q), blk(CB_NA_K), blk(CB_NA_V), small((1, LANES)), small((1, LANES)),
                  small((LANES, LANES))],
        out_specs=pl.BlockSpec((1, ctx, LANES), lambda hp, b: (b, 0, hp)),
        out_shape=jax.ShapeDtypeStruct((B, ctx, MIX_W), BF16),
        compiler_params=_cp(("arbitrary", "arbitrary")),
        name="ctx_attn",
    )(pc, pc, pc, qg, kg, bd)


def _log_sigmoid(x):
    return jnp.minimum(x, 0.0) - jnp.log(1.0 + jnp.exp(-jnp.abs(x)))


def _split_dot(a, b_f32, a_is_const):
    if a_is_const:
        hi = b_f32.astype(BF16)
        lo = (b_f32 - hi.astype(F32)).astype(BF16)
        return _dot(a, hi) + _dot(a, lo)
    hi = a.astype(BF16)
    lo = (a - hi.astype(F32)).astype(BF16)
    return _dot(hi, b_f32) + _dot(lo, b_f32)


def _mlstm_kernel(*refs):
    (qf, kf, vf, gf, gtf, qb, kb, vb, gb, gtb, qc, kc, vc, gc, gtc, tri_ref, trit_ref,
     hf_ref, hb_ref, hcf_ref, hcb_ref, c_scr, m_scr) = refs
    cl = ML_CHUNK
    s = pl.program_id(1)

    @pl.when(s == 0)
    def _():
        c_scr[...] = jnp.zeros_like(c_scr)
        m_scr[...] = jnp.zeros_like(m_scr)

    tri, trit = tri_ref[...], trit_ref[...]
    row_i = lax.broadcasted_iota(jnp.int32, (cl, cl), 0)
    col_i = lax.broadcasted_iota(jnp.int32, (cl, cl), 1)
    lane = lax.broadcasted_iota(jnp.int32, (cl, LANES), 1)
    ones_col = jnp.where(lane == 0, 1.0, 0.0).astype(BF16)

    def stream(d, q_ref, k_ref, v_ref, g_ref, gt_ref, out_ref):
        g = g_ref[0]
        gt = gt_ref[0]
        lf_cols = _log_sigmoid(g)
        lf_rows = _log_sigmoid(gt)
        if d == 0:
            b_cols = _split_dot(tri, lf_cols, True)
            b_rows = _split_dot(lf_rows, trit, False)
            mask = col_i <= row_i
        else:
            b_cols = _split_dot(trit, lf_cols, True)
            b_rows = _split_dot(lf_rows, tri, False)
            mask = col_i >= row_i
        outs = []
        for h in range(ML_HEADS):
            ci = (2 * d) * ML_HEADS + h
            cf = (2 * d + 1) * ML_HEADS + h
            q = q_ref[0][:, h * ML_HD:(h + 1) * ML_HD]
            k = k_ref[0][:, h * ML_HD:(h + 1) * ML_HD]
            v_aug = jnp.concatenate([v_ref[0][:, h * ML_HD:(h + 1) * ML_HD], ones_col], axis=1)
            b_col = b_cols[:, cf:cf + 1]
            b_row = b_rows[cf:cf + 1, :]
            li_row = gt[ci:ci + 1, :]
            m_prev = m_scr[d, h][0:1, 0:1]
            ct = c_scr[d, h]
            dm = jnp.where(mask, b_col - b_row + li_row, -jnp.inf)
            inter = b_col + m_prev
            m_t = jnp.maximum(jnp.max(dm, axis=-1, keepdims=True), inter)
            sc = _dot_nt(q, k) * jnp.exp(dm - m_t)
            gi = jnp.exp(inter - m_t)
            nd = _dot(sc.astype(BF16), v_aug) + gi * _dot(q, ct.astype(BF16))
            num = nd[:, :ML_HD]
            den = nd[:, ML_HD:ML_HD + 1]
            outs.append(num / jnp.maximum(jnp.abs(den), jnp.exp(-m_t)))
            b_last = b_row[:, cl - 1:cl] if d == 0 else b_row[:, 0:1]
            tail = b_last - b_row + li_row
            m_new = jnp.maximum(b_last + m_prev, jnp.max(tail, axis=-1, keepdims=True))
            w_row = jnp.exp(tail - m_new)
            decay = jnp.exp(b_last + m_prev - m_new)
            kt = k.astype(F32).T
            c_scr[d, h] = decay * ct + _dot((kt * w_row).astype(BF16), v_aug)
            m_scr[d, h] = jnp.broadcast_to(m_new, (SUBLANES, LANES))
        out_ref[0] = jnp.concatenate(outs, axis=1).astype(BF16)

    @pl.when(s == 0)
    def _():
        stream(0, qc, kc, vc, gc, gtc, hcf_ref)
        stream(1, qc, kc, vc, gc, gtc, hcb_ref)

    @pl.when(s > 0)
    def _():
        stream(0, qf, kf, vf, gf, gtf, hf_ref)
        stream(1, qb, kb, vb, gb, gtb, hb_ref)


def _mlstm(mlq, mlk, p, g, mlqc, mlkc, pc, gc):
    B, L, _ = mlq.shape
    cl = ML_CHUNK
    assert pc.shape[1] == cl
    n = L // cl
    gt = jnp.swapaxes(g[..., :ML_GATES], 1, 2)
    gtc = jnp.swapaxes(gc[..., :ML_GATES], 1, 2)
    tri = jnp.asarray(np.tril(np.ones((cl, cl))), BF16)
    trit = jnp.asarray(np.triu(np.ones((cl, cl))), BF16)

    fwd = lambda s: jnp.maximum(s - 1, 0)
    bwd = lambda s: jnp.clip(n - s, 0, n - 1)

    def specs(sel, vcol):
        return [pl.BlockSpec((1, cl, MIX_W), lambda b, s: (b, sel(s), 0)),
                pl.BlockSpec((1, cl, MIX_W), lambda b, s: (b, sel(s), 0)),
                pl.BlockSpec((1, cl, MIX_W), lambda b, s: (b, sel(s), vcol)),
                pl.BlockSpec((1, cl, LANES), lambda b, s: (b, sel(s), 0)),
                pl.BlockSpec((1, ML_GATES, cl), lambda b, s: (b, 0, sel(s)))]

    zero = lambda s: 0
    const = pl.BlockSpec((cl, cl), lambda b, s: (0, 0))
    return pl.pallas_call(
        _mlstm_kernel,
        grid=(B, n + 1),
        in_specs=specs(fwd, CB_ML_V) + specs(bwd, CB_ML_V) + specs(zero, CB_ML_V) + [const, const],
        out_specs=[pl.BlockSpec((1, cl, MIX_W), lambda b, s: (b, fwd(s), 0)),
                   pl.BlockSpec((1, cl, MIX_W), lambda b, s: (b, bwd(s), 0)),
                   pl.BlockSpec((1, cl, MIX_W), lambda b, s: (b, 0, 0)),
                   pl.BlockSpec((1, cl, MIX_W), lambda b, s: (b, 0, 0))],
        out_shape=[jax.ShapeDtypeStruct((B, L, MIX_W), BF16)] * 2 + [jax.ShapeDtypeStruct((B, cl, MIX_W), BF16)] * 2,
        scratch_shapes=[pltpu.VMEM((2, ML_HEADS, ML_HD, 2 * ML_HD), F32),
                        pltpu.VMEM((2, ML_HEADS, SUBLANES, LANES), F32)],
        compiler_params=_cp(("arbitrary", "arbitrary")),
        name="mlstm",
    )(mlq, mlk, p, g, gt, mlq, mlk, p, g, gt, mlqc, mlkc, pc, gc, gtc, tri, trit)


def _filter_kernel(z_ref, w1_ref, b1_ref, w2_ref, b2_ref, w3_ref, fr_ref, dl_ref, j_ref, hf_ref, hb_ref, *, length):
    z = z_ref[...]
    fr = fr_ref[...]
    h1 = jnp.sin(fr[0:1] * (_dot_hi(z, w1_ref[...]) + b1_ref[...]))
    h2 = jnp.sin(fr[1:2] * (_dot_hi(h1, w2_ref[...]) + b2_ref[...]))
    h = _dot_hi(h2, w3_ref[...])
    t = z[:, 0:1]
    decay = jnp.exp(-t * dl_ref[...])
    fwd, bwd = [], []
    for o in range(HY_ORDER):
        fwd.append(h[:, (2 * o) * MIX_W:(2 * o + 1) * MIX_W] * decay)
        bwd.append(h[:, (2 * o + 1) * MIX_W:(2 * o + 2) * MIX_W] * decay)
    hf_ref[...] = jnp.concatenate(fwd, axis=1)
    hb = jnp.concatenate(bwd, axis=1)
    jm = j_ref[...]
    a = hb.astype(BF16)
    r1 = hb - a.astype(F32)
    b = r1.astype(BF16)
    c = (r1 - b.astype(F32)).astype(BF16)
    hb_ref[...] = _dot(jm, a) + _dot(jm, b) + _dot(jm, c)


def _hyena_filter(L, w1, b1, w2, b2, w3, freq):
    tb = FFT_N2
    nb = L // tb
    t = (jnp.arange(L, dtype=F32) / L)[:, None]
    bands = jnp.linspace(1e-4, HY_BANDS - 1, HY_BANDS, dtype=F32)
    ang = 2 * math.pi * t * bands
    z = jnp.concatenate([t, jnp.cos(ang), jnp.sin(ang), jnp.zeros((L, LANES - HY_EMB), F32)], axis=-1)
    deltas = np.abs(np.linspace(math.log(HY_TARGET) / HY_SLOW_DECAY, math.log(HY_TARGET) / HY_FAST_DECAY, MIX_W,
                                dtype=np.float32)).reshape(1, MIX_W)
    w1p = jnp.zeros((LANES, HY_FFN), F32).at[:HY_EMB].set(w1.astype(F32))
    jm = jnp.asarray(np.eye(tb)[::-1].copy(), BF16)
    width = HY_ORDER * MIX_W
    full = lambda a: pl.BlockSpec(a.shape, lambda i: (0,) * a.ndim)
    args = (z, w1p, b1.reshape(1, -1).astype(F32), w2.astype(F32), b2.reshape(1, -1).astype(F32),
            w3.astype(F32), freq.astype(F32), jnp.asarray(deltas), jm)
    hf, hb_rev = pl.pallas_call(
        functools.partial(_filter_kernel, length=L),
        grid=(nb,),
        in_specs=[pl.BlockSpec((tb, LANES), lambda i: (i, 0))] + [full(a) for a in args[1:]],
        out_specs=[pl.BlockSpec((tb, width), lambda i: (i, 0)),
                   pl.BlockSpec((tb, width), lambda i: (nb - 1 - i, 0))],
        out_shape=[jax.ShapeDtypeStruct((L, width), F32)] * 2,
        compiler_params=_cp(("arbitrary",)),
        name="hy_filter",
    )(*args)
    return hf, hb_rev


def _fft_consts(n1, n1v):
    jb = FFT_JB
    th1 = 2 * np.pi * np.outer(np.arange(n1), np.arange(n1v)) / n1
    f1 = np.kron(np.concatenate([np.cos(th1), -np.sin(th1)], axis=0), np.eye(jb))
    i1 = np.kron(np.concatenate([np.cos(th1.T), -np.sin(th1.T)], axis=1), np.eye(jb))
    th2 = 2 * np.pi * np.outer(np.arange(FFT_N2), np.arange(FFT_N2)) / FFT_N2
    c2, s2 = np.cos(th2), np.sin(th2)
    f2 = np.block([[c2, s2], [-s2, c2]])
    i2 = np.block([[c2, -s2], [s2, c2]])
    prod = (jnp.arange(n1, dtype=jnp.int32)[:, None] * jnp.arange(FFT_N2, dtype=jnp.int32)[None, :]).astype(F32)
    tht = prod * (2 * math.pi / (n1 * FFT_N2))
    twr = jnp.broadcast_to(jnp.cos(tht)[:, :, None], (n1, FFT_N2, LANES))
    twi = jnp.broadcast_to(-jnp.sin(tht)[:, :, None], (n1, FFT_N2, LANES))
    return (jnp.asarray(f1, BF16), jnp.asarray(i1, BF16), jnp.asarray(f2, BF16), jnp.asarray(i2, BF16), twr, twi)


def _fft_outer_kernel(u_ref, f1_ref, a_ref):
    n1v, jb, ct = u_ref.shape[1:]
    n1 = a_ref.shape[1]
    u = u_ref[0].reshape(n1v * jb, ct).astype(BF16)
    res = _dot(f1_ref[...], u)
    half = n1 * jb
    a_ref[0] = _pack(res[:half], res[half:]).reshape(n1, jb, ct)


def _fft_outer(u4, f1, n1):
    B, n1v, _, C = u4.shape
    ct = MIX_W
    return pl.pallas_call(
        _fft_outer_kernel,
        grid=(B, FFT_N2 // FFT_JB, C // ct),
        in_specs=[pl.BlockSpec((1, n1v, FFT_JB, ct), lambda b, j, c: (b, 0, j, c)),
                  pl.BlockSpec(f1.shape, lambda b, j, c: (0, 0))],
        out_specs=pl.BlockSpec((1, n1, FFT_JB, ct), lambda b, j, c: (b, 0, j, c)),
        out_shape=jax.ShapeDtypeStruct((B, n1, FFT_N2, C), U32),
        compiler_params=_cp(("arbitrary", "arbitrary", "arbitrary")),
        name="fft_outer",
    )(u4, f1)


def _twiddle(re, im, twr, twi, conj):
    reps = re.shape[1] // LANES
    tr = jnp.concatenate([twr] * reps, axis=1)
    ti = jnp.concatenate([twi] * reps, axis=1)
    if conj:
        ti = -ti
    return re * tr - im * ti, re * ti + im * tr


def _filter_spec_kernel(a_ref, twr_ref, twi_ref, f2_ref, kf_ref):
    re, im = _unpack(a_ref[0, 0])
    re, im = _twiddle(re, im, twr_ref[0], twi_ref[0], False)
    x = _dot(f2_ref[...], jnp.concatenate([re, im], axis=0).astype(BF16))
    kf_ref[0] = _pack(x[:FFT_N2], x[FFT_N2:])


def _filter_spec(a, twr, twi, f2):
    _, n1, _, C = a.shape
    return pl.pallas_call(
        _filter_spec_kernel,
        grid=(n1,),
        in_specs=[pl.BlockSpec((1, 1, FFT_N2, C), lambda k: (0, k, 0, 0)),
                  pl.BlockSpec((1, FFT_N2, LANES), lambda k: (k, 0, 0)),
                  pl.BlockSpec((1, FFT_N2, LANES), lambda k: (k, 0, 0)),
                  pl.BlockSpec(f2.shape, lambda k: (0, 0))],
        out_specs=pl.BlockSpec((1, FFT_N2, C), lambda k: (k, 0, 0)),
        out_shape=jax.ShapeDtypeStruct((n1, FFT_N2, C), U32),
        compiler_params=_cp(("arbitrary",)),
        name="filter_spec",
    )(a, twr, twi, f2)


def _fft_mid_kernel(a_ref, kf_ref, twr_ref, twi_ref, f2_ref, i2_ref, d_ref):
    nb = a_ref.shape[0]
    twr, twi = twr_ref[0], twi_ref[0]
    res, ims = [], []
    for b in range(nb):
        re, im = _unpack(a_ref[b, 0])
        re, im = _twiddle(re, im, twr, twi, False)
        res.append(re)
        ims.append(im)
    stacked = jnp.concatenate([jnp.concatenate(res, axis=1), jnp.concatenate(ims, axis=1)], axis=0)
    x = _dot(f2_ref[...], stacked.astype(BF16))
    xr, xi = x[:FFT_N2], x[FFT_N2:]
    kr, ki = _unpack(kf_ref[0])
    kr = jnp.concatenate([kr] * nb, axis=1)
    ki = jnp.concatenate([ki] * nb, axis=1)
    zr = xr * kr - xi * ki
    zi = xr * ki + xi * kr
    y = _dot(i2_ref[...], jnp.concatenate([zr, zi], axis=0).astype(BF16))
    yr, yi = _twiddle(y[:FFT_N2], y[FFT_N2:], twr, twi, True)
    c = a_ref.shape[-1]
    for b in range(nb):
        d_ref[b, 0] = _pack(yr[:, b * c:(b + 1) * c], yi[:, b * c:(b + 1) * c])


def _fft_mid(a, kf, order, twr, twi, f2, i2):
    B, n1, _, C = a.shape
    return pl.pallas_call(
        _fft_mid_kernel,
        grid=(n1,),
        in_specs=[pl.BlockSpec((B, 1, FFT_N2, C), lambda k: (0, k, 0, 0)),
                  pl.BlockSpec((1, FFT_N2, C), lambda k: (k, 0, order)),
                  pl.BlockSpec((1, FFT_N2, LANES), lambda k: (k, 0, 0)),
                  pl.BlockSpec((1, FFT_N2, LANES), lambda k: (k, 0, 0)),
                  pl.BlockSpec(f2.shape, lambda k: (0, 0)),
                  pl.BlockSpec(i2.shape, lambda k: (0, 0))],
        out_specs=pl.BlockSpec((B, 1, FFT_N2, C), lambda k: (0, k, 0, 0)),
        out_shape=jax.ShapeDtypeStruct((B, n1, FFT_N2, C), U32),
        compiler_params=_cp(("arbitrary",)),
        name="fft_mid",
    )(a, kf, twr, twi, f2, i2)


def _fft_inv_outer_kernel(d_ref, i1_ref, u_ref, x_ref, bias_ref, y_ref, *, scale):
    n1, jb, ct = d_ref.shape[1:]
    n1v = y_ref.shape[1]
    re, im = _unpack(d_ref[0].reshape(n1 * jb, ct))
    conv = _dot(i1_ref[...], jnp.concatenate([re, im], axis=0).astype(BF16)) * scale
    u = u_ref[0].reshape(n1v * jb, ct)
    y = x_ref[0].reshape(n1v * jb, ct) * (conv + bias_ref[...] * u)
    y_ref[0] = y.reshape(n1v, jb, ct)


def _fft_inv_outer(d, i1, u4, x4, bias, n_fft):
    B, n1, _, C = d.shape
    n1v = u4.shape[1]
    ct = MIX_W
    sig = pl.BlockSpec((1, n1v, FFT_JB, ct), lambda b, j, c: (b, 0, j, c))
    return pl.pallas_call(
        functools.partial(_fft_inv_outer_kernel, scale=1.0 / n_fft),
        grid=(B, FFT_N2 // FFT_JB, C // ct),
        in_specs=[pl.BlockSpec((1, n1, FFT_JB, ct), lambda b, j, c: (b, 0, j, c)),
                  pl.BlockSpec(i1.shape, lambda b, j, c: (0, 0)),
                  sig, sig,
                  pl.BlockSpec((1, ct), lambda b, j, c: (0, c))],
        out_specs=sig,
        out_shape=jax.ShapeDtypeStruct(u4.shape, F32),
        compiler_params=_cp(("arbitrary", "arbitrary", "arbitrary")),
        name="fft_inv_outer",
    )(d, i1, u4, x4, bias)


def _hyena(hv, hx1, hx2, w1, b1, w2, b2, w3, freq, bias):
    B, L, C = hv.shape
    lp = max(L, 8 * FFT_N2)
    n_fft = 2 * lp
    n1 = n_fft // FFT_N2
    n1v = lp // FFT_N2
    f1s, i1s, f2, i2, twr, twi = _fft_consts(n1, n1v)
    f1full = _fft_consts(n1, n1)[0]

    hf, hb_rev = _hyena_filter(L, w1, b1, w2, b2, w3, freq)
    width = HY_ORDER * C
    kern = jnp.concatenate([hf, jnp.zeros((n_fft - 2 * L + 1, width), F32), hb_rev[:L - 1]], axis=0)
    ka = _fft_outer(kern.reshape(1, n1, FFT_N2, width), f1full, n1)
    kf = _filter_spec(ka, twr, twi, f2)

    def pad(t):
        if lp != L:
            t = jnp.concatenate([t, jnp.zeros((B, lp - L, C), t.dtype)], axis=1)
        return t.reshape(B, n1v, FFT_N2, C)

    y = pad(hv)
    for o, xg in enumerate((hx1, hx2)):
        a = _fft_outer(y, f1s, n1)
        d = _fft_mid(a, kf, o, twr, twi, f2, i2)
        y = _fft_inv_outer(d, i1s, y, pad(xg), bias[o].reshape(1, C).astype(F32), n_fft)
    return y.reshape(B, lp, C)[:, :L]


ROUTE_STRIDE = 32


def _route_lanes(v):
    v = v.astype(F32)
    parts = []
    for j in range(EPG):
        col = v[..., j::EPG]
        pad = jnp.zeros(col.shape[:-1] + (ROUTE_STRIDE - N_GROUPS,), F32)
        parts.append(jnp.concatenate([col, pad], axis=-1))
    return jnp.concatenate(parts, axis=-1)


def _route_tile(scores, rb):
    tm = scores.shape[0]
    lane = lax.broadcasted_iota(jnp.int32, (tm, LANES), 1)
    sel = scores + rb
    a = [sel] + [pltpu.roll(sel, LANES - ROUTE_STRIDE * j, axis=1) for j in range(1, EPG)]
    u = [scores] + [pltpu.roll(scores, LANES - ROUTE_STRIDE * j, axis=1) for j in range(1, EPG)]
    hi01, lo01 = jnp.maximum(a[0], a[1]), jnp.minimum(a[0], a[1])
    hi23, lo23 = jnp.maximum(a[2], a[3]), jnp.minimum(a[2], a[3])
    gs = jnp.maximum(hi01, hi23) + jnp.maximum(jnp.minimum(hi01, hi23), jnp.maximum(lo01, lo23))
    gs = jnp.where(lane < N_GROUPS, gs, -jnp.inf)
    mx = jnp.max(gs, axis=-1, keepdims=True)
    gidx = jnp.min(jnp.where(gs == mx, lane, LANES), axis=-1, keepdims=True)
    pick = lane == gidx
    v = [jnp.sum(jnp.where(pick, t, 0.0), axis=-1, keepdims=True) for t in a]
    w = [jnp.sum(jnp.where(pick, t, 0.0), axis=-1, keepdims=True) for t in u]
    b1, i1, w1 = v[0], jnp.zeros_like(gidx), w[0]
    for j in range(1, EPG):
        t = v[j] > b1
        b1, i1, w1 = jnp.where(t, v[j], b1), jnp.where(t, j, i1), jnp.where(t, w[j], w1)
    b2, i2, w2 = jnp.full_like(b1, -jnp.inf), jnp.zeros_like(gidx), jnp.zeros_like(w1)
    for j in range(EPG):
        t = jnp.logical_and(i1 != j, v[j] > b2)
        b2, i2, w2 = jnp.where(t, v[j], b2), jnp.where(t, j, i2), jnp.where(t, w[j], w2)
    den = w1 + w2
    e1 = (gidx * EPG + i1).astype(F32)
    e2 = (gidx * EPG + i2).astype(F32)
    return jnp.where(lane == 0, w1 / den, jnp.where(lane == 1, w2 / den,
                     jnp.where(lane == 2, e1, jnp.where(lane == 3, e2, 0.0))))


def _merge_kernel(na_ref, hf_ref, hb_ref, og_ref, sc_ref, hy_ref, gain_ref, wout_ref, x_ref, m2_ref, m3_ref, m4_ref,
                  gffn_ref, rwh_ref, rwl_ref, rb_ref, bd64_ref, bd128_ref, xo_ref, hp_ref, s_ref):
    gain = gain_ref[...]
    y_ml = jax.nn.sigmoid(og_ref[0].astype(F32)) * (hf_ref[0].astype(F32) + hb_ref[0].astype(F32))
    parts = ((na_ref[0].astype(F32), bd64_ref), (y_ml, bd128_ref), (sc_ref[0].astype(F32), bd64_ref),
             (hy_ref[0], bd64_ref))
    acc = None
    for n, (y, bd) in enumerate(parts):
        yn = _group_rms(y, bd[...], gain[:, n * MIX_W:(n + 1) * MIX_W]).astype(BF16)
        t = _dot(yn, wout_ref[n * MIX_W:(n + 1) * MIX_W, :])
        acc = t if acc is None else acc + t
    x = x_ref[0] + m2_ref[0] * acc
    xo_ref[0] = x
    ms = jnp.mean(x * x, axis=-1, keepdims=True)
    h2 = (x * lax.rsqrt(ms + NORM_EPS) * gffn_ref[...]) * (1.0 + m4_ref[0]) + m3_ref[0]
    hi = h2.astype(BF16)
    lo = (h2 - hi.astype(F32)).astype(BF16)
    logits = _dot(hi, rwh_ref[...]) + _dot(hi, rwl_ref[...]) + _dot(lo, rwh_ref[...])
    s_ref[0] = _route_tile(jax.nn.sigmoid(logits), rb_ref[...])
    half = D_MODEL // 2
    hf32 = hi.astype(F32)
    hp_ref[0] = _pack(hf32[:, :half], hf32[:, half:])


def _merge(y_na, hf, hb, p, y_sc, y_hy, out_gain, w_out, x, m2, m3, m4, g_ffn, rwh, rwl, rb):
    B, L, D = x.shape
    tm = min(256, L)
    bd64 = _block_diag_mean(MIX_W, 64)
    bd128 = _block_diag_mean(MIX_W, 128)
    row = lambda w: pl.BlockSpec((1, tm, w), lambda b, i: (b, i, 0))
    mod = pl.BlockSpec((1, 1, D), lambda b, i: (b, 0, 0))
    full = lambda a: pl.BlockSpec(a.shape, lambda b, i: (0,) * a.ndim)
    return pl.pallas_call(
        _merge_kernel,
        grid=(B, L // tm),
        in_specs=[row(MIX_W), row(MIX_W), row(MIX_W),
                  pl.BlockSpec((1, tm, MIX_W), lambda b, i: (b, i, CB_ML_O)),
                  row(MIX_W), row(MIX_W), full(out_gain), full(w_out), row(D), mod, mod, mod,
                  full(g_ffn), full(rwh), full(rwl), full(rb), full(bd64), full(bd128)],
        out_specs=[row(D), row(D // 2), row(LANES)],
        out_shape=[jax.ShapeDtypeStruct((B, L, D), F32), jax.ShapeDtypeStruct((B, L, D // 2), U32),
                   jax.ShapeDtypeStruct((B, L, LANES), F32)],
        compiler_params=_cp(("arbitrary", "arbitrary")),
        name="merge",
    )(y_na, hf, hb, p, y_sc, y_hy, out_gain, w_out, x, m2, m3, m4, g_ffn, rwh, rwl, rb, bd64, bd128)


def _plan_kernel(r_ref, tri_ref, rank_ref, cnt_ref, carry):
    @pl.when(pl.program_id(0) == 0)
    def _():
        carry[...] = jnp.zeros_like(carry)

    r = r_ref[...]
    lane = lax.broadcasted_iota(jnp.int32, r.shape, 1)
    o0 = lane == r[:, 2:3].astype(jnp.int32)
    o1 = lane == r[:, 3:4].astype(jnp.int32)
    o = jnp.where(jnp.logical_or(o0, o1), 1.0, 0.0)
    tot = _dot(tri_ref[...], o.astype(BF16)) + carry[0:1, :]
    r0 = jnp.sum(jnp.where(o0, tot, 0.0), axis=-1, keepdims=True)
    r1 = jnp.sum(jnp.where(o1, tot, 0.0), axis=-1, keepdims=True)
    rank_ref[...] = jnp.where(lane == 0, r0, jnp.where(lane == 1, r1, 0.0))
    carry[...] = carry[...] + jnp.sum(o, axis=0, keepdims=True)
    cnt_ref[...] = carry[...]


def _plan(route):
    T = route.shape[0]
    tm = 512
    tri = jnp.asarray(np.tril(np.ones((tm, tm)), -1), BF16)
    rank, cnt = pl.pallas_call(
        _plan_kernel,
        grid=(T // tm,),
        in_specs=[pl.BlockSpec((tm, LANES), lambda i: (i, 0)), pl.BlockSpec((tm, tm), lambda i: (0, 0))],
        out_specs=[pl.BlockSpec((tm, LANES), lambda i: (i, 0)), pl.BlockSpec((SUBLANES, LANES), lambda i: (0, 0))],
        out_shape=[jax.ShapeDtypeStruct((T, LANES), F32), jax.ShapeDtypeStruct((SUBLANES, LANES), F32)],
        scratch_shapes=[pltpu.VMEM((SUBLANES, LANES), F32)],
        compiler_params=_cp(("arbitrary",)),
        name="moe_plan",
    )(route, tri)
    A = T * TOP_K
    counts = cnt[0, :N_EXPERTS].astype(jnp.int32)
    padded = (counts + MOE_BLOCK - 1) // MOE_BLOCK * MOE_BLOCK
    pend = jnp.cumsum(padded)
    pstart = pend - padded
    expert = route[:, 2:2 + TOP_K].astype(jnp.int32)
    onehot = expert[:, :, None] == jnp.arange(N_EXPERTS, dtype=jnp.int32)[None, None, :]
    base = jnp.sum(jnp.where(onehot, pstart[None, None, :], 0), axis=-1)
    dest = base + rank[:, :TOP_K].astype(jnp.int32)
    n_rows = (A + MOE_BLOCK - 1) // MOE_BLOCK * MOE_BLOCK + N_EXPERTS * MOE_BLOCK
    n_blocks = n_rows // MOE_BLOCK
    starts = jnp.arange(n_blocks, dtype=jnp.int32) * MOE_BLOCK
    block_e = jnp.minimum(jnp.sum((pend[None, :] <= starts[:, None]).astype(jnp.int32), axis=1), N_EXPERTS - 1)
    n_used = (pend[-1] // MOE_BLOCK).astype(jnp.int32).reshape(1)
    return dest, block_e.astype(jnp.int32), n_used, n_rows


def _scatter_kernel(dest_ref, h_ref, xs_in_ref, xs_ref, sem, *, tm):
    del xs_in_ref

    def copy(t, d):
        return pltpu.make_async_copy(h_ref.at[pl.ds(t, 1), :], xs_ref.at[pl.ds(d, 1), :], sem)

    def issue(t, c):
        for k in range(TOP_K):
            copy(t, dest_ref[0, 0, TOP_K * t + k]).start()
        return c

    def drain(t, c):
        for k in range(TOP_K):
            copy(t, dest_ref[0, 0, TOP_K * t + k]).wait()
        return c

    lax.fori_loop(0, tm, issue, 0, unroll=DMA_UNROLL)
    lax.fori_loop(0, tm, drain, 0, unroll=DMA_UNROLL)


def _scatter(h2p, dest, xs):
    T, W = h2p.shape
    tm = 256
    return pl.pallas_call(
        functools.partial(_scatter_kernel, tm=tm),
        grid=(T // tm,),
        in_specs=[pl.BlockSpec((1, 1, TOP_K * tm), lambda i: (i, 0, 0), memory_space=pltpu.SMEM),
                  pl.BlockSpec((tm, W), lambda i: (i, 0)),
                  pl.BlockSpec(memory_space=pl.ANY)],
        out_specs=pl.BlockSpec(memory_space=pl.ANY),
        out_shape=jax.ShapeDtypeStruct(xs.shape, xs.dtype),
        scratch_shapes=[pltpu.SemaphoreType.DMA(())],
        input_output_aliases={2: 0},
        compiler_params=_cp(("arbitrary",)),
        name="moe_scatter",
    )(dest.reshape(T // tm, 1, TOP_K * tm), h2p, xs)


def _ffn_kernel(be_ref, nu_ref, x_ref, wg_ref, wu_ref, wd_ref, y_ref):
    del be_ref
    used = pl.program_id(0) < nu_ref[0]
    half = D_MODEL // 2

    @pl.when(used)
    def _():
        lo, hi = _unpack(x_ref[...])
        lo, hi = lo.astype(BF16), hi.astype(BF16)
        g = _dot(lo, wg_ref[0, :half, :]) + _dot(hi, wg_ref[0, half:, :])
        u = _dot(lo, wu_ref[0, :half, :]) + _dot(hi, wu_ref[0, half:, :])
        h = (g * jax.nn.sigmoid(g) * u).astype(BF16)
        y = _dot(h, wd_ref[0])
        y_ref[...] = _pack(y[:, :half], y[:, half:])

    @pl.when(jnp.logical_not(used))
    def _():
        y_ref[...] = jnp.zeros_like(y_ref)


def _ffn(xs, block_e, n_used, wg, wu, wd):
    n_rows, W = xs.shape
    nb = n_rows // MOE_BLOCK
    grid_spec = pltpu.PrefetchScalarGridSpec(
        num_scalar_prefetch=2,
        grid=(nb,),
        in_specs=[pl.BlockSpec((MOE_BLOCK, W), lambda i, be, nu: (i, 0)),
                  pl.BlockSpec((1, D_MODEL, EXPERT_FF), lambda i, be, nu: (be[i], 0, 0)),
                  pl.BlockSpec((1, D_MODEL, EXPERT_FF), lambda i, be, nu: (be[i], 0, 0)),
                  pl.BlockSpec((1, EXPERT_FF, D_MODEL), lambda i, be, nu: (be[i], 0, 0))],
        out_specs=pl.BlockSpec((MOE_BLOCK, W), lambda i, be, nu: (i, 0)),
    )
    return pl.pallas_call(
        _ffn_kernel,
        grid_spec=grid_spec,
        out_shape=jax.ShapeDtypeStruct((n_rows, W), U32),
        compiler_params=_cp(("arbitrary",)),
        name="moe_ffn",
    )(block_e, n_used, xs, wg, wu, wd)


def _combine_kernel(dest_ref, yb_ref, g_ref, x_ref, m5_ref, o_ref, buf, sem, *, tm):
    def copy(t, k, d):
        return pltpu.make_async_copy(yb_ref.at[pl.ds(d, 1), :], buf.at[k, pl.ds(t, 1), :], sem)

    def issue(t, c):
        for k in range(TOP_K):
            copy(t, k, dest_ref[0, 0, TOP_K * t + k]).start()
        return c

    def drain(t, c):
        for k in range(TOP_K):
            copy(t, k, dest_ref[0, 0, TOP_K * t + k]).wait()
        return c

    lax.fori_loop(0, tm, issue, 0, unroll=DMA_UNROLL)
    lax.fori_loop(0, tm, drain, 0, unroll=DMA_UNROLL)
    g = g_ref[...]
    a_lo, a_hi = _unpack(buf[0])
    b_lo, b_hi = _unpack(buf[1])
    g0, g1 = g[:, 0:1], g[:, 1:2]
    half = D_MODEL // 2
    m5 = m5_ref[0]
    o_ref[:, :half] = x_ref[:, :half] + m5[:, :half] * (g0 * a_lo + g1 * b_lo)
    o_ref[:, half:] = x_ref[:, half:] + m5[:, half:] * (g0 * a_hi + g1 * b_hi)


def _combine(yb, dest, gp, x2d, m5, rows_per_batch):
    T, D = x2d.shape
    tm = min(256, rows_per_batch)
    tpb = rows_per_batch // tm
    return pl.pallas_call(
        functools.partial(_combine_kernel, tm=tm),
        grid=(T // tm,),
        in_specs=[pl.BlockSpec((1, 1, TOP_K * tm), lambda i: (i, 0, 0), memory_space=pltpu.SMEM),
                  pl.BlockSpec(memory_space=pl.ANY),
                  pl.BlockSpec((tm, LANES), lambda i: (i, 0)),
                  pl.BlockSpec((tm, D), lambda i: (i, 0)),
                  pl.BlockSpec((1, 1, D), lambda i: (i // tpb, 0, 0))],
        out_specs=pl.BlockSpec((tm, D), lambda i: (i, 0)),
        out_shape=jax.ShapeDtypeStruct((T, D), F32),
        scratch_shapes=[pltpu.VMEM((TOP_K, tm, D // 2), U32), pltpu.SemaphoreType.DMA(())],
        compiler_params=_cp(("arbitrary",)),
        name="moe_combine",
    )(dest.reshape(T // tm, 1, TOP_K * tm), yb, gp, x2d, m5)


def _moe(streams, wg, wu, wd):
    route = jnp.concatenate([s[2].reshape(-1, LANES) for s in streams], axis=0)
    dest, block_e, n_used, n_rows = _plan(route)
    xs = jnp.zeros((n_rows, D_MODEL // 2), U32)
    off = 0
    for s in streams:
        n = s[1].shape[0] * s[1].shape[1]
        xs = _scatter(s[1].reshape(n, D_MODEL // 2), dest[off:off + n], xs)
        off += n
    yb = _ffn(xs, block_e, n_used, wg, wu, wd)
    outs, off = [], 0
    for x, _, _, m5 in streams:
        B, L, D = x.shape
        n = B * L
        o = _combine(yb, dest[off:off + n], route[off:off + n], x.reshape(n, D), m5, L)
        outs.append(o.reshape(B, L, D))
        off += n
    return outs


def _layer(x, xc, mod, modc, need_ctx, norm_mix, norm_ffn, w_in, gate_b, w_out, out_norm, na_q, na_k, na_rpb,
           ml_cw, ml_cb, sc_w, hy_cw, hy_cb, hy_w1, hy_b1, hy_w2, hy_b2, hy_w3, hy_freq, hy_bias,
           router_w, router_b, wg, wu, wd, rope_tabs):
    B, L, D = x.shape
    w_main = jnp.concatenate([w_in[:, :GATE_COL0], w_in[:, GATE_COL0 + ML_GATES:]], axis=1).astype(BF16)
    w_gate = jnp.zeros((D, LANES), BF16).at[:, :ML_GATES].set(w_in[:, GATE_COL0:GATE_COL0 + ML_GATES].astype(BF16))
    b_gate = jnp.zeros((1, LANES), F32).at[0, :ML_GATES].set(gate_b.astype(F32))
    gain_mix = norm_mix.reshape(1, D).astype(F32)

    p, g = _inproj(x, mod[0], mod[1], gain_mix, w_main, w_gate, b_gate)
    pc, gc = _inproj(xc, modc[0], modc[1], gain_mix, w_main, w_gate, b_gate)

    mlq, mlk, y_sc, hv, hx1, hx2 = _prep(p, ml_cw, ml_cb, sc_w, hy_cw, hy_cb, rope_tabs)
    mlqc, mlkc, yc_sc, hvc, hx1c, hx2c = _prep(pc, ml_cw, ml_cb, sc_w, hy_cw, hy_cb, None)

    y_na = _na(p, pc, na_rpb, na_q, na_k)
    hf, hb, hcf, hcb = _mlstm(mlq, mlk, p, g, mlqc, mlkc, pc, gc)
    y_hy = _hyena(hv, hx1, hx2, hy_w1, hy_b1, hy_w2, hy_b2, hy_w3, hy_freq, hy_bias)

    w_out_b = w_out.astype(BF16)
    out_gain = out_norm.reshape(1, D).astype(F32)
    g_ffn = norm_ffn.reshape(1, D).astype(F32)
    rw = _route_lanes(router_w)
    rwh = rw.astype(BF16)
    rwl = (rw - rwh.astype(F32)).astype(BF16)
    rb = _route_lanes(router_b).reshape(1, LANES)

    x1, h2p, sc = _merge(y_na, hf, hb, p, y_sc, y_hy, out_gain, w_out_b, x, mod[2], mod[3], mod[4], g_ffn, rwh, rwl,
                         rb)
    streams = [(x1, h2p, sc, mod[5])]
    if need_ctx:
        yc_na = _ctx_attn(pc, na_q, na_k)
        yc_hy = _hyena(hvc, hx1c, hx2c, hy_w1, hy_b1, hy_w2, hy_b2, hy_w3, hy_freq, hy_bias)
        xc1, h2pc, scc = _merge(yc_na, hcf, hcb, pc, yc_sc, yc_hy, out_gain, w_out_b, xc, modc[2], modc[3], modc[4],
                                g_ffn, rwh, rwl, rb)
        streams.append((xc1, h2pc, scc, modc[5]))
    outs = _moe(streams, wg.astype(BF16), wu.astype(BF16), wd.astype(BF16))
    return outs[0], (outs[1] if need_ctx else None)


def kernel(x, c, ctx, c_ctx, w_ada, b_ada, norm_mix, norm_ffn, w_in, mlstm_gate_bias, w_out, out_norm, na_q_norm, na_k_norm, na_rpb, mlstm_conv_w, mlstm_conv_b, sconv_w, hyena_conv_w, hyena_conv_b, hyena_f_w1, hyena_f_b1, hyena_f_w2, hyena_f_b2, hyena_f_w3, hyena_f_freq, hyena_bias, router_w, router_bias, moe_w_gate, moe_w_up, moe_w_down):
    B, L, D = x.shape
    depth = w_ada.shape[0]
    xc = ctx
    s = jnp.zeros((SUBLANES, D), F32)
    s = s.at[:B].set(jax.nn.silu(c)).at[B].set(jax.nn.silu(c_ctx))
    rope_tabs = _rope_tables(L)
    for l in range(depth):
        need_ctx = l < depth - 1
        m = _ada(s, w_ada[l], b_ada[l])
        mod = [m[:B, n * D:(n + 1) * D].reshape(B, 1, D) for n in range(6)]
        modc = [jnp.broadcast_to(m[B, n * D:(n + 1) * D].reshape(1, 1, D), (B, 1, D)) for n in range(6)]
        x, xc = _layer(x, xc, mod, modc, need_ctx, norm_mix[l], norm_ffn[l], w_in[l], mlstm_gate_bias[l], w_out[l],
                       out_norm[l], na_q_norm[l], na_k_norm[l], na_rpb[l], mlstm_conv_w[l], mlstm_conv_b[l],
                       sconv_w[l], hyena_conv_w[l], hyena_conv_b[l], hyena_f_w1[l], hyena_f_b1[l], hyena_f_w2[l],
                       hyena_f_b2[l], hyena_f_w3[l], hyena_f_freq[l], hyena_bias[l], router_w, router_bias,
                       moe_w_gate[l], moe_w_up[l], moe_w_down[l], rope_tabs)
    return x
```
